```python
import jax, jax.numpy as jnp
from jax import lax
import numpy as np

D_MODEL = 1024
BATCH = 8
SEQ = 2048
DEPTH = 4

CTX_LEN = 256
GRID_W = 64
N_MOD = 9
D_FF = 2816
EPS = 1e-6
GDN_HEADS = 4
GDN_HEAD_DIM = 128
GDN_WIDTH = GDN_HEADS * GDN_HEAD_DIM
CONV_K = 5
CONV_PAD = CONV_K // 2
CHUNK = 64
MLA_HEADS = 8
MLA_NOPE = 64
MLA_ROPE = 32
MLA_V = 64
MLA_WIDTH = MLA_HEADS * MLA_V
Q_RANK = 384
KV_RANK = 256
ROPE_BASE = 10000.0
AXIS_DIM = MLA_ROPE // 2
Q_BLOCK = 128
MLA_SCALE = (MLA_NOPE + MLA_ROPE) ** -0.5
MIX_WIDTH = GDN_WIDTH + MLA_WIDTH
OFF_Q = 0
OFF_K = GDN_WIDTH
OFF_V = 2 * GDN_WIDTH
OFF_Z = 3 * GDN_WIDTH
OFF_A = 4 * GDN_WIDTH
OFF_B = OFF_A + 2 * GDN_HEADS
OFF_CQ = OFF_B + 2 * GDN_HEADS
OFF_CKV = OFF_CQ + Q_RANK
OFF_KR = OFF_CKV + KV_RANK
IN_COLS = OFF_KR + MLA_ROPE

kernel_name = "hybrid_gdn_mla_macaron_dit"


def rmsnorm(x, g):
    xf = x.astype(jnp.float32)
    y = xf * lax.rsqrt(jnp.mean(xf * xf, axis=-1, keepdims=True) + EPS)
    return (y * g.astype(jnp.float32)).astype(x.dtype)


def l2norm(x):
    xf = x.astype(jnp.float32)
    return (xf * lax.rsqrt(jnp.sum(xf * xf, axis=-1, keepdims=True) + EPS)).astype(x.dtype)


def pre_mod(x, g, m, slot):
    return rmsnorm(x, g) * (1 + m[:, None, 3 * slot + 1]) + m[:, None, 3 * slot]


def post_residual(x, y, g, m, slot, weight):
    return x + weight * m[:, None, 3 * slot + 2] * rmsnorm(y, g)


def ffn_sublayer(x, m, slot, g_pre, g_post, wg, wu, wd):
    h = pre_mod(x, g_pre, m, slot)
    y = (jax.nn.silu(h @ wg) * (h @ wu)) @ wd
    return post_residual(x, y, g_post, m, slot, 0.5)


def short_conv(u, w):
    C = u.shape[-1]
    y = lax.conv_general_dilated(u, w[:, None, :].astype(u.dtype), window_strides=(1,),
                                 padding=[(CONV_PAD, CONV_PAD)],
                                 dimension_numbers=('NWC', 'WIO', 'NWC'),
                                 feature_group_count=C)
    return jax.nn.silu(y)


def gated_delta_chunked(q, k, v, g, beta, S0):
    out_dtype = v.dtype
    q, k, v, g, beta = [u.astype(jnp.float32) for u in (q, k, v, g, beta)]
    B2, T, H, dk = q.shape
    dv = v.shape[-1]
    n = T // CHUNK

    def blocks(u):
        return u.reshape(B2, n, CHUNK, H, -1).transpose(0, 3, 1, 2, 4)

    qb, kb, vb = blocks(q), blocks(k), blocks(v)
    gb = g.reshape(B2, n, CHUNK, H).transpose(0, 3, 1, 2)
    bb = beta.reshape(B2, n, CHUNK, H).transpose(0, 3, 1, 2)
    G = jnp.cumsum(gb, axis=-1)
    diff = G[..., :, None] - G[..., None, :]
    idx = jnp.arange(CHUNK)
    incl = idx[:, None] >= idx[None, :]
    strict = idx[:, None] > idx[None, :]
    decay = jnp.where(incl, jnp.exp(jnp.where(incl, diff, 0.0)), 0.0)
    kk = jnp.einsum('bhnid,bhnjd->bhnij', kb, kb)
    A = jnp.where(strict, bb[..., :, None] * kk * decay, 0.0)
    eyeA = A + jnp.eye(CHUNK, dtype=A.dtype)
    eG = jnp.exp(G)
    rhs = jnp.concatenate([bb[..., None] * vb, (bb * eG)[..., None] * kb], axis=-1)
    sol = lax.linalg.triangular_solve(eyeA, rhs, left_side=True, lower=True,
                                      unit_diagonal=True)
    u0, w = sol[..., :dv], sol[..., dv:]
    qk = jnp.einsum('bhnid,bhnjd->bhnij', qb, kb) * decay
    q_dec = qb * eG[..., None]
    G_last = G[..., -1]
    k_dec = kb * jnp.exp(G_last[..., None] - G)[..., None]
    chunk_dec = jnp.exp(G_last)
    xs = tuple(jnp.moveaxis(u, 2, 0) for u in (q_dec, k_dec, qk, u0, w, chunk_dec))

    def step(S, inp):
        qd, kd, qkc, u0c, wc, gl = inp
        U = u0c - jnp.einsum('bhcd,bhde->bhce', wc, S)
        o = jnp.einsum('bhcd,bhde->bhce', qd, S) + jnp.einsum('bhij,bhje->bhie', qkc, U)
        S = gl[..., None, None] * S + jnp.einsum('bhcd,bhce->bhde', kd, U)
        return S, o

    S_fin, o = lax.scan(step, S0, xs)
    o = o.transpose(1, 0, 3, 2, 4).reshape(B2, T, H, dv)
    return o.astype(out_dtype), S_fin


def gdn_group(qkv, z, a, b, conv_w, a_log, dt_bias, out_norm, L):
    B, T, _ = qkv.shape
    qkv = jnp.concatenate([short_conv(qkv[:, :L], conv_w), short_conv(qkv[:, L:], conv_w)], axis=1)
    q = qkv[..., :GDN_WIDTH].reshape(B, T, GDN_HEADS, GDN_HEAD_DIM)
    k = qkv[..., GDN_WIDTH:2 * GDN_WIDTH].reshape(B, T, GDN_HEADS, GDN_HEAD_DIM)
    v = qkv[..., 2 * GDN_WIDTH:].reshape(B, T, GDN_HEADS, GDN_HEAD_DIM)
    q = l2norm(q) * (GDN_HEAD_DIM ** -0.5)
    k = l2norm(k)
    a = a.reshape(B, T, 2, GDN_HEADS)
    b = b.reshape(B, T, 2, GDN_HEADS)
    g = -jnp.exp(a_log) * jax.nn.softplus(a + dt_bias)
    beta = jax.nn.sigmoid(b)

    def flip(u):
        return u[:, ::-1]

    def bidir(qs, ks, vs, gs, bs, S0):
        qq = jnp.concatenate([qs, flip(qs)], 0)
        kk = jnp.concatenate([ks, flip(ks)], 0)
        vv = jnp.concatenate([vs, flip(vs)], 0)
        gg = jnp.concatenate([gs[:, :, 0], flip(gs[:, :, 1])], 0)
        be = jnp.concatenate([bs[:, :, 0], flip(bs[:, :, 1])], 0)
        o, S = gated_delta_chunked(qq, kk, vv, gg, be, S0)
        return o[:B] + flip(o[B:]), S

    S_zero = jnp.zeros((2 * B, GDN_HEADS, GDN_HEAD_DIM, GDN_HEAD_DIM), jnp.float32)
    o_c, S_c = bidir(q[:, :L], k[:, :L], v[:, :L], g[:, :L], beta[:, :L], S_zero)
    o_l, _ = bidir(q[:, L:], k[:, L:], v[:, L:], g[:, L:], beta[:, L:], S_c)
    o = jnp.concatenate([o_c, o_l], axis=1)
    o = rmsnorm(o, out_norm) * jax.nn.silu(z.reshape(B, T, GDN_HEADS, GDN_HEAD_DIM))
    return o.reshape(B, T, GDN_WIDTH)


def rope2d(x, cos, sin):
    half = MLA_ROPE // 2
    x1, x2 = x[..., :half], x[..., half:]
    return jnp.concatenate([x1 * cos - x2 * sin, x2 * cos + x1 * sin], axis=-1).astype(x.dtype)


def attend(q, k, v):
    s = jnp.einsum('bqhd,bkhd->bhqk', q, k).astype(jnp.float32) * MLA_SCALE
    p = jax.nn.softmax(s, axis=-1)
    return jnp.einsum('bhqk,bkhd->bqhd', p.astype(v.dtype), v)


def mla_group(c_q, c_kv, k_rope, q_norm, kv_norm, w_uq, w_ukv, cos, sin, L):
    B, T, _ = c_q.shape
    q = (rmsnorm(c_q, q_norm) @ w_uq).reshape(B, T, MLA_HEADS, MLA_NOPE + MLA_ROPE)
    kv = (rmsnorm(c_kv, kv_norm) @ w_ukv).reshape(B, T, MLA_HEADS, MLA_NOPE + MLA_V)
    q_nope, q_rope = q[..., :MLA_NOPE], q[..., MLA_NOPE:]
    k_nope, v = kv[..., :MLA_NOPE], kv[..., MLA_NOPE:]
    q_rope = jnp.concatenate([q_rope[:, :L], rope2d(q_rope[:, L:], cos[:, None], sin[:, None])], axis=1)
    k_rope = jnp.concatenate([k_rope[:, :L], rope2d(k_rope[:, L:], cos, sin)], axis=1)
    qf = jnp.concatenate([q_nope, q_rope], axis=-1)
    kf = jnp.concatenate([k_nope, jnp.broadcast_to(k_rope[:, :, None], (B, T, MLA_HEADS, MLA_ROPE))], axis=-1)
    o_c = attend(qf[:, :L], kf[:, :L], v[:, :L])
    S = T - L
    nb = S // Q_BLOCK
    ql = qf[:, L:].reshape(B, nb, Q_BLOCK, MLA_HEADS, MLA_NOPE + MLA_ROPE).transpose(1, 0, 2, 3, 4)
    o_l = lax.map(lambda qb: attend(qb, kf, v), ql)
    o_l = o_l.transpose(1, 0, 2, 3, 4).reshape(B, S, MLA_HEADS, MLA_V)
    return jnp.concatenate([o_c, o_l], axis=1).reshape(B, T, MLA_WIDTH)


def mixer(h_c, h_l, w_in, conv_w, a_log, dt_bias, out_norm, q_norm, kv_norm, w_uq, w_ukv, w_out, cos, sin):
    L = h_c.shape[1]
    h = jnp.concatenate([h_c, h_l], axis=1)
    p = h @ w_in
    o_gdn = gdn_group(p[..., OFF_Q:OFF_Z], p[..., OFF_Z:OFF_A], p[..., OFF_A:OFF_B],
                      p[..., OFF_B:OFF_CQ], conv_w, a_log, dt_bias, out_norm, L)
    o_mla = mla_group(p[..., OFF_CQ:OFF_CKV], p[..., OFF_CKV:OFF_KR], p[..., OFF_KR:IN_COLS],
                      q_norm, kv_norm, w_uq, w_ukv, cos, sin, L)
    o = jnp.concatenate([o_gdn, o_mla], axis=-1) @ w_out
    return o[:, :L], o[:, L:]


def setup_inputs(seed: int = 0) -> dict:
    key = jax.random.key(seed)
    ks = jax.random.split(key, 24)
    f32 = jnp.float32

    def nrm(k, shape, fan):
        return jax.random.normal(k, shape, f32) * (fan ** -0.5)

    def gain(k, shape):
        return 1.0 + 0.05 * jax.random.normal(k, shape, f32)

    dt = jnp.exp(jax.random.uniform(ks[10], (DEPTH, 2, GDN_HEADS), f32,
                                    minval=np.log(0.001), maxval=np.log(0.1)))
    return {
        "x": jax.random.normal(ks[0], (BATCH, SEQ, D_MODEL), f32),
        "c": jax.random.normal(ks[1], (BATCH, D_MODEL), f32),
        "ctx": jax.random.normal(ks[2], (BATCH, CTX_LEN, D_MODEL), f32),
        "c_ctx": jax.random.normal(ks[3], (D_MODEL,), f32),
        "w_ada": 0.5 * nrm(ks[4], (DEPTH, D_MODEL, N_MOD * D_MODEL), D_MODEL),
        "b_ada": 0.02 * jax.random.normal(ks[5], (DEPTH, N_MOD * D_MODEL), f32),
        "norm_pre": gain(ks[6], (DEPTH, 3, D_MODEL)),
        "norm_post": gain(ks[7], (DEPTH, 3, D_MODEL)),
        "ffn_w_gate": nrm(ks[8], (DEPTH, 2, D_MODEL, D_FF), D_MODEL),
        "ffn_w_up": nrm(ks[9], (DEPTH, 2, D_MODEL, D_FF), D_MODEL),
        "ffn_w_down": nrm(ks[11], (DEPTH, 2, D_FF, D_MODEL), D_FF),
        "w_in": nrm(ks[12], (DEPTH, D_MODEL, IN_COLS), D_MODEL),
        "gdn_conv": nrm(ks[13], (DEPTH, CONV_K, 3 * GDN_WIDTH), CONV_K),
        "gdn_a_log": jnp.log(jax.random.uniform(ks[14], (DEPTH, 2, GDN_HEADS), f32, minval=1.0, maxval=16.0)),
        "gdn_dt_bias": dt + jnp.log(-jnp.expm1(-dt)),
        "gdn_out_norm": gain(ks[15], (DEPTH, GDN_HEAD_DIM)),
        "mla_q_norm": gain(ks[16], (DEPTH, Q_RANK)),
        "mla_kv_norm": gain(ks[17], (DEPTH, KV_RANK)),
        "mla_w_uq": nrm(ks[18], (DEPTH, Q_RANK, MLA_HEADS * (MLA_NOPE + MLA_ROPE)), Q_RANK),
        "mla_w_ukv": nrm(ks[19], (DEPTH, KV_RANK, MLA_HEADS * (MLA_NOPE + MLA_V)), KV_RANK),
        "w_out": nrm(ks[20], (DEPTH, MIX_WIDTH, D_MODEL), MIX_WIDTH),
    }


def reference(x, c, ctx, c_ctx, w_ada, b_ada, norm_pre, norm_post, ffn_w_gate, ffn_w_up,
              ffn_w_down, w_in, gdn_conv, gdn_a_log, gdn_dt_bias, gdn_out_norm, mla_q_norm,
              mla_kv_norm, mla_w_uq, mla_w_ukv, w_out):
    B, S, D = x.shape
    ROWS = S // GRID_W
    row = jnp.repeat(jnp.arange(ROWS), GRID_W).astype(jnp.float32)
    col = jnp.tile(jnp.arange(GRID_W), ROWS).astype(jnp.float32)
    inv_freq = jnp.power(ROPE_BASE, -jnp.arange(0, AXIS_DIM, 2, dtype=jnp.float32) / AXIS_DIM)
    ang = jnp.concatenate([row[:, None] * inv_freq, col[:, None] * inv_freq], axis=-1)
    cos, sin = jnp.cos(ang), jnp.sin(ang)

    s_lat = jax.nn.silu(c)
    s_ctx = jax.nn.silu(c_ctx)[None]
    xc, xl = ctx, x
    for l in range(DEPTH):
        last = l == DEPTH - 1
        m_l = (s_lat @ w_ada[l] + b_ada[l]).reshape(B, N_MOD, D)
        m_c = (s_ctx @ w_ada[l] + b_ada[l]).reshape(1, N_MOD, D)
        xc = ffn_sublayer(xc, m_c, 0, norm_pre[l, 0], norm_post[l, 0], ffn_w_gate[l, 0], ffn_w_up[l, 0], ffn_w_down[l, 0])
        xl = ffn_sublayer(xl, m_l, 0, norm_pre[l, 0], norm_post[l, 0], ffn_w_gate[l, 0], ffn_w_up[l, 0], ffn_w_down[l, 0])
        hc = pre_mod(xc, norm_pre[l, 1], m_c, 1)
        hl = pre_mod(xl, norm_pre[l, 1], m_l, 1)
        oc, ol = mixer(hc, hl, w_in[l], gdn_conv[l], gdn_a_log[l], gdn_dt_bias[l], gdn_out_norm[l],
                       mla_q_norm[l], mla_kv_norm[l], mla_w_uq[l], mla_w_ukv[l], w_out[l], cos, sin)
        xl = post_residual(xl, ol, norm_post[l, 1], m_l, 1, 1.0)
        xl = ffn_sublayer(xl, m_l, 2, norm_pre[l, 2], norm_post[l, 2], ffn_w_gate[l, 1], ffn_w_up[l, 1], ffn_w_down[l, 1])
        if not last:
            xc = post_residual(xc, oc, norm_post[l, 1], m_c, 1, 1.0)
            xc = ffn_sublayer(xc, m_c, 2, norm_pre[l, 2], norm_post[l, 2], ffn_w_gate[l, 1], ffn_w_up[l, 1], ffn_w_down[l, 1])
    return xl
```

```python
import functools

import jax
import jax.numpy as jnp
from jax import lax
from jax.experimental import pallas as pl
from jax.experimental.pallas import tpu as pltpu

F32 = jnp.float32
BF16 = jnp.bfloat16

EPS = 1e-6
N_MOD = 9
GDN_HEADS = 4
GDN_HEAD_DIM = 128
GDN_WIDTH = GDN_HEADS * GDN_HEAD_DIM
CONV_K = 5
CONV_PAD = CONV_K // 2
MLA_HEADS = 8
MLA_NOPE = 64
MLA_ROPE = 32
MLA_V = 64
Q_RANK = 384
KV_RANK = 256
ROPE_BASE = 10000.0
GRID_W = 64
MLA_SCALE = (MLA_NOPE + MLA_ROPE) ** -0.5

LANES = 128
ROPE_HALF = MLA_ROPE // 2
ROPE_LANE0 = MLA_NOPE
GATE_LANE0 = ROPE_LANE0 + MLA_ROPE
INV_BASE = 8
VMEM_LIMIT = 56 * 1024 * 1024


def _pick(n, candidates):
    for cand in candidates:
        if n % cand == 0:
            return cand
    raise ValueError(f"no tile for {n}")


def _sigmoid(x):
    return 1.0 / (1.0 + jnp.exp(-x))


def _silu(x):
    return x * _sigmoid(x)


def _rms(x, gain):
    return x * lax.rsqrt(jnp.mean(x * x, axis=-1, keepdims=True) + EPS) * gain


def _dot(a, b):
    return jnp.dot(a, b, preferred_element_type=F32)


def _dot_nt(a, b):
    return lax.dot_general(a, b, (((1,), (1,)), ((), ())), preferred_element_type=F32)


def _const_spec(shape):
    zeros = (0,) * len(shape)
    return pl.BlockSpec(shape, lambda *_: zeros, pipeline_mode=pl.Buffered(1))


def _params(sem):
    return pltpu.CompilerParams(dimension_semantics=sem, vmem_limit_bytes=VMEM_LIMIT)


def _ada_kernel(s_ref, w_ref, b_ref, o_ref):
    s = _silu(s_ref[...])
    o_ref[0] = jnp.dot(s, w_ref[0], precision=lax.Precision.HIGHEST,
                       preferred_element_type=F32) + b_ref[0]


def ada_call(cond, w_ada, b_ada):
    depth, d, n = w_ada.shape
    rows = cond.shape[0]
    tn = _pick(n, (1024, 512, 256, 128))
    return pl.pallas_call(
        _ada_kernel,
        grid=(depth, n // tn),
        in_specs=[pl.BlockSpec((rows, d), lambda l, j: (0, 0)),
                  pl.BlockSpec((1, d, tn), lambda l, j: (l, 0, j)),
                  pl.BlockSpec((1, 1, tn), lambda l, j: (l, 0, j))],
        out_specs=pl.BlockSpec((1, rows, tn), lambda l, j: (l, 0, j)),
        out_shape=jax.ShapeDtypeStruct((depth, rows, n), F32),
        compiler_params=_params(("arbitrary", "arbitrary")),
        name="ada",
    )(cond, w_ada, b_ada.reshape(depth, 1, n))


def _mod_rows(ml_ref, mc_ref, slot, is_ctx):
    out = []
    for j in range(3):
        lat = ml_ref[0, pl.ds(3 * slot + j, 1), :]
        ctx = mc_ref[0, pl.ds(3 * slot + j, 1), :]
        out.append(jnp.where(is_ctx, ctx, lat))
    return out


def _is_ctx_rows(tm, n_ctx):
    rows = pl.program_id(1) * tm + lax.broadcasted_iota(jnp.int32, (tm, 1), 0)
    return rows < n_ctx


def _mod_specs(n_ctx_row, d):
    return [pl.BlockSpec((1, N_MOD, d), lambda b, i: (b, 0, 0)),
            pl.BlockSpec((1, N_MOD, d), lambda b, i: (n_ctx_row, 0, 0))]


def _ffn_kernel(x_ref, ml_ref, mc_ref, gpre_ref, gpost_ref, wg_ref, wu_ref, wd_ref, o_ref,
                *, slot, n_ctx, fc):
    tm = x_ref.shape[1]
    x = x_ref[0]
    shift, scale, gate = _mod_rows(ml_ref, mc_ref, slot, _is_ctx_rows(tm, n_ctx))
    h = (_rms(x, gpre_ref[...]) * (1.0 + scale) + shift).astype(BF16)
    y = jnp.zeros(x.shape, F32)
    for j in range(wg_ref.shape[1] // fc):
        cols = pl.ds(j * fc, fc)
        act = _silu(_dot(h, wg_ref[:, cols])) * _dot(h, wu_ref[:, cols])
        y = y + _dot(act.astype(BF16), wd_ref[cols, :])
    o_ref[0] = x + 0.5 * gate * _rms(y, gpost_ref[...])


def ffn_call(x, mod, gpre, gpost, wg, wu, wd, *, slot, n_ctx, tm):
    b, t, d = x.shape
    dff = wg.shape[1]
    fc = _pick(dff, (256, 128))
    kern = functools.partial(_ffn_kernel, slot=slot, n_ctx=n_ctx, fc=fc)
    return pl.pallas_call(
        kern,
        grid=(b, t // tm),
        in_specs=[pl.BlockSpec((1, tm, d), lambda b, i: (b, i, 0))]
        + _mod_specs(b, d)
        + [_const_spec((1, d)), _const_spec((1, d)),
           _const_spec((d, dff)), _const_spec((d, dff)), _const_spec((dff, d))],
        out_specs=pl.BlockSpec((1, tm, d), lambda b, i: (b, i, 0)),
        out_shape=jax.ShapeDtypeStruct(x.shape, F32),
        compiler_params=_params(("arbitrary", "arbitrary")),
        name="ffn",
    )(x, mod, mod, gpre, gpost, wg, wu, wd)


def _rope(x, c_ref, s1_ref, s2_ref):
    return (x * c_ref[...] + pltpu.roll(x, ROPE_HALF, 1) * s1_ref[...]
            + pltpu.roll(x, LANES - ROPE_HALF, 1) * s2_ref[...])


def _mixin_kernel(x_ref, ml_ref, mc_ref, gpre_ref, win_ref, qn_ref, kvn_ref, wuq_ref, wuk_ref,
                  wuv_ref, c_ref, s1_ref, s2_ref,
                  qkvz_ref, small_ref, q_ref, k_ref, v_ref, *, n_ctx):
    tm = x_ref.shape[1]
    shift, scale, _ = _mod_rows(ml_ref, mc_ref, 1, _is_ctx_rows(tm, n_ctx))
    h = (_rms(x_ref[0], gpre_ref[...]) * (1.0 + scale) + shift).astype(BF16)
    n_gdn = qkvz_ref.shape[2]
    qkvz_ref[0] = _dot(h, win_ref[:, pl.ds(0, n_gdn)])
    rest = _dot(h, win_ref[:, pl.ds(n_gdn, Q_RANK + KV_RANK + LANES)])
    c_q = rest[:, :Q_RANK]
    c_kv = rest[:, Q_RANK:Q_RANK + KV_RANK]
    small = _rope(rest[:, Q_RANK + KV_RANK:], c_ref, s1_ref, s2_ref)
    small_ref[0] = small
    lane = lax.broadcasted_iota(jnp.int32, (1, LANES), 1)
    k_rope = jnp.where((lane >= ROPE_LANE0) & (lane < GATE_LANE0), small, 0.0)

    q = _dot(_rms(c_q, qn_ref[...]).astype(BF16), wuq_ref[...]) * MLA_SCALE
    ckv_n = _rms(c_kv, kvn_ref[...]).astype(BF16)
    k = _dot(ckv_n, wuk_ref[...])
    for hd in range(MLA_HEADS):
        cols = pl.ds(hd * LANES, LANES)
        blk = slice(hd * LANES, (hd + 1) * LANES)
        q_ref[0, :, cols] = _rope(q[:, blk], c_ref, s1_ref, s2_ref).astype(BF16)
        k_ref[0, :, cols] = (k[:, blk] + k_rope).astype(BF16)
    v_ref[0] = _dot(ckv_n, wuv_ref[...]).astype(BF16)


def mixin_call(x, mod, gpre, win, qn, kvn, wuq, wuk, wuv, rope_c, rope_s1, rope_s2, *, n_ctx, tm):
    b, t, d = x.shape
    n_gdn = 4 * GDN_WIDTH
    hw = MLA_HEADS * LANES
    vw = MLA_HEADS * MLA_V
    tok = lambda w: pl.BlockSpec((1, tm, w), lambda b, i: (b, i, 0))
    tab = pl.BlockSpec((tm, LANES), lambda b, i: (i, 0))
    kern = functools.partial(_mixin_kernel, n_ctx=n_ctx)
    return pl.pallas_call(
        kern,
        grid=(b, t // tm),
        in_specs=[tok(d)] + _mod_specs(b, d)
        + [_const_spec((1, d)), _const_spec(win.shape), _const_spec((1, Q_RANK)),
           _const_spec((1, KV_RANK)), _const_spec(wuq.shape), _const_spec(wuk.shape),
           _const_spec(wuv.shape), tab, tab, tab],
        out_specs=[tok(n_gdn), tok(LANES), tok(hw), tok(hw), tok(vw)],
        out_shape=[jax.ShapeDtypeStruct((b, t, n_gdn), F32),
                   jax.ShapeDtypeStruct((b, t, LANES), F32),
                   jax.ShapeDtypeStruct((b, t, hw), BF16),
                   jax.ShapeDtypeStruct((b, t, hw), BF16),
                   jax.ShapeDtypeStruct((b, t, vw), BF16)],
        compiler_params=_params(("arbitrary", "arbitrary")),
        name="mixin",
    )(x, mod, mod, gpre, win, qn, kvn, wuq, wuk, wuv, rope_c, rope_s1, rope_s2)


def _split3(x):
    hi = x.astype(BF16)
    return hi, (x - hi.astype(F32)).astype(BF16)


def _mm3(a, b):
    ah, al = _split3(a)
    bh, bl = _split3(b)
    return _dot(ah, bh) + (_dot(ah, bl) + _dot(al, bh))


def _tri_inv(a, eye, blk_masks):
    a0 = jnp.where(blk_masks[0], a, 0.0)
    a2 = _mm3(a0, a0)
    a4 = _mm3(a2, a2)
    x = (jnp.where(eye, 1.0, 0.0) - a0 + a2) - _mm3(a0, a2)
    x = x + _mm3(x, a4)
    for inner, outer in zip(blk_masks[:-1], blk_masks[1:]):
        a_off = jnp.where(outer & jnp.logical_not(inner), a, 0.0)
        x = x - _mm3(_mm3(x, a_off), x)
    return x


def _gdn_kernel(alog_ref, dtb_ref, q_ref, k_ref, v_ref, z_ref, cq_ref, ck_ref, cv_ref, gate_ref,
                onorm_ref, o_ref,
                q_s, k_s, v_s, gb_s, u0_s, w_s, qd_s, kdt_s, qk_s, of_s, ob_s, *, n_ctx, ck):
    t = q_ref.shape[1]
    dk = q_ref.shape[2]
    nc = t // ck
    nl = n_ctx // ck
    hd = pl.program_id(1)

    row = lax.broadcasted_iota(jnp.int32, (t, 1), 0)

    def conv(u_ref, w_ref):
        u = u_ref[0]
        acc = u * w_ref[pl.ds(CONV_PAD, 1), :]
        for s in range(-CONV_PAD, CONV_PAD + 1):
            if s == 0:
                continue
            src = row + s
            valid = (src >= 0) & (src < t) & ((src >= n_ctx) == (row >= n_ctx))
            shifted = pltpu.roll(u, (-s) % t, 0)
            acc = acc + jnp.where(valid, shifted, 0.0) * w_ref[pl.ds(CONV_PAD + s, 1), :]
        return _silu(acc)

    def l2n(u):
        return u * lax.rsqrt(jnp.sum(u * u, axis=-1, keepdims=True) + EPS)

    qf = l2n(conv(q_ref, cq_ref)) * (dk ** -0.5)
    kf = l2n(conv(k_ref, ck_ref))
    vf = conv(v_ref, cv_ref)
    gates = gate_ref[0, 0]
    cols = []
    for d in range(2):
        a = gates[:, d:d + 1] + dtb_ref[d, hd]
        softplus = jnp.maximum(a, 0.0) + jnp.log(1.0 + jnp.exp(-jnp.abs(a)))
        cols.append(-jnp.exp(alog_ref[d, hd]) * softplus)
    for d in range(2):
        cols.append(_sigmoid(gates[:, 2 + d:3 + d]))
    gb = jnp.concatenate(cols, axis=1)
    for c in range(nc):
        rows = slice(c * ck, (c + 1) * ck)
        q_s[c] = qf[rows]
        k_s[c] = kf[rows]
        v_s[c] = vf[rows]
        gb_s[c] = gb[rows]

    ii = lax.broadcasted_iota(jnp.int32, (ck, ck), 0)
    jj = lax.broadcasted_iota(jnp.int32, (ck, ck), 1)
    eye = ii == jj
    blk_masks = []
    size = INV_BASE
    while size <= ck:
        sh = size.bit_length() - 1
        blk_masks.append((ii >> sh) == (jj >> sh))
        size *= 2

    for d in range(2):
        incl = (ii >= jj) if d == 0 else (ii <= jj)
        strict = (ii > jj) if d == 0 else (ii < jj)
        incl_t = (jj >= ii) if d == 0 else (jj <= ii)

        def chunk_body(c, carry, d=d, incl=incl, strict=strict, incl_t=incl_t):
            q = q_s[c]
            k = k_s[c]
            v = v_s[c]
            g_col = gb_s[c][:, d:d + 1]
            beta = gb_s[c][:, 2 + d:3 + d]
            g_row = jnp.sum(jnp.where(eye, g_col, 0.0), axis=0, keepdims=True)
            cum_col = jnp.sum(jnp.where(incl, g_row, 0.0), axis=1, keepdims=True)
            cum_row = jnp.sum(jnp.where(incl_t, g_col, 0.0), axis=0, keepdims=True)
            total = jnp.sum(g_col, axis=0, keepdims=True)
            decay = jnp.where(incl, jnp.exp(jnp.where(incl, cum_col - cum_row, 0.0)), 0.0)
            kb = k.astype(BF16)
            kk = _dot_nt(kb, kb)
            a = jnp.where(strict, beta * kk * decay, 0.0)
            tinv = _tri_inv(a, eye, blk_masks)
            e_cum = jnp.exp(cum_col)
            rhs = jnp.concatenate([beta * v, (beta * e_cum) * k], axis=1)
            sol = _mm3(tinv, rhs)
            u0_s[d, c] = sol[:, :dk]
            w_s[d, c] = sol[:, dk:]
            qk_s[d, c] = _dot_nt(q.astype(BF16), kb) * decay
            qd_s[d, c] = q * e_cum
            kdt_s[d, c] = (k * jnp.exp(total - cum_col)).T
            return carry

        lax.fori_loop(0, nc, chunk_body, 0)

    def scan_body(i, carry):
        s_f, s_b = carry
        c_f = i
        c_b = jnp.where(i < nl, nl - 1 - i, nc - 1 + nl - i)
        new = []
        for d, c, s, o_s in ((0, c_f, s_f, of_s), (1, c_b, s_b, ob_s)):
            sb = s.astype(BF16)
            u = u0_s[d, c] - _dot(w_s[d, c].astype(BF16), sb)
            ub = u.astype(BF16)
            o_s[c] = _dot(qd_s[d, c].astype(BF16), sb) + _dot(qk_s[d, c].astype(BF16), ub)
            total = jnp.sum(gb_s[c][:, d:d + 1], axis=0, keepdims=True)
            new.append(jnp.exp(total) * s + _dot(kdt_s[d, c].astype(BF16), ub))
        return tuple(new)

    zero = jnp.zeros((dk, dk), F32)
    lax.fori_loop(0, nc, scan_body, (zero, zero))

    for c in range(nc):
        o = _rms(of_s[c] + ob_s[c], onorm_ref[...])
        o_ref[0, pl.ds(c * ck, ck), :] = (o * _silu(z_ref[0, pl.ds(c * ck, ck), :])).astype(BF16)


def gdn_call(qkvz, gates, conv_w, a_log, dt_bias, out_norm, *, n_ctx, ck):
    b, t, _ = qkvz.shape
    h, dk = GDN_HEADS, GDN_HEAD_DIM
    nc = t // ck
    col = lambda off: pl.BlockSpec((1, t, dk), lambda b, hd: (b, 0, off + hd))
    cw = lambda off: pl.BlockSpec((CONV_K, dk), lambda b, hd: (0, off + hd))
    smem = pl.BlockSpec(memory_space=pltpu.SMEM)
    chunked = lambda w: pltpu.VMEM((nc, ck, w), F32)
    both = lambda r, w: pltpu.VMEM((2, nc, r, w), F32)
    kern = functools.partial(_gdn_kernel, n_ctx=n_ctx, ck=ck)
    return pl.pallas_call(
        kern,
        grid=(b, h),
        in_specs=[smem, smem, col(0), col(h), col(2 * h), col(3 * h), cw(0), cw(h), cw(2 * h),
                  pl.BlockSpec((1, 1, t, 4), lambda b, hd: (b, hd, 0, 0)),
                  pl.BlockSpec((1, dk), lambda b, hd: (0, 0))],
        out_specs=pl.BlockSpec((1, t, dk), lambda b, hd: (b, 0, hd)),
        out_shape=jax.ShapeDtypeStruct((b, t, h * dk), BF16),
        scratch_shapes=[chunked(dk), chunked(dk), chunked(dk), chunked(4),
                        both(ck, dk), both(ck, dk), both(ck, dk), both(dk, ck), both(ck, ck),
                        chunked(dk), chunked(dk)],
        compiler_params=_params(("arbitrary", "arbitrary")),
        name="gdn",
    )(a_log, dt_bias, qkvz, qkvz, qkvz, qkvz, conv_w, conv_w, conv_w, gates, out_norm)


def _attn_kernel(q_ref, k_ref, v_ref, o_ref, *, n_ctx):
    tq = q_ref.shape[1]
    t = k_ref.shape[1]
    group = 2 * LANES // MLA_V
    lane = lax.broadcasted_iota(jnp.int32, (1, group * MLA_V), 1)

    def attend(n_keys):
        for g0 in range(0, MLA_HEADS, group):
            vg = v_ref[0, pl.ds(0, n_keys), pl.ds(g0 * MLA_V, group * MLA_V)]
            out = jnp.zeros((tq, group * MLA_V), F32)
            for hd in range(g0, g0 + group):
                cols = pl.ds(hd * LANES, LANES)
                s = _dot_nt(q_ref[0, :, cols], k_ref[0, pl.ds(0, n_keys), cols])
                p = jnp.exp(s - jnp.max(s, axis=-1, keepdims=True))
                denom = jnp.sum(p, axis=-1, keepdims=True)
                pv = _dot(p.astype(BF16), vg) / denom
                j = hd - g0
                out = jnp.where((lane >= j * MLA_V) & (lane < (j + 1) * MLA_V), pv, out)
            o_ref[0, :, pl.ds(g0 * MLA_V, group * MLA_V)] = out.astype(BF16)

    is_ctx = pl.program_id(1) * tq < n_ctx

    @pl.when(is_ctx)
    def _():
        attend(n_ctx)

    @pl.when(jnp.logical_not(is_ctx))
    def _():
        attend(t)


def attn_call(q, k, v, *, n_ctx, tq):
    b, t, hw = q.shape
    vw = v.shape[2]
    kern = functools.partial(_attn_kernel, n_ctx=n_ctx)
    return pl.pallas_call(
        kern,
        grid=(b, t // tq),
        in_specs=[pl.BlockSpec((1, tq, hw), lambda b, i: (b, i, 0)),
                  pl.BlockSpec((1, t, hw), lambda b, i: (b, 0, 0)),
                  pl.BlockSpec((1, t, vw), lambda b, i: (b, 0, 0))],
        out_specs=pl.BlockSpec((1, tq, vw), lambda b, i: (b, i, 0)),
        out_shape=jax.ShapeDtypeStruct((b, t, vw), BF16),
        compiler_params=_params(("arbitrary", "arbitrary")),
        name="attn",
    )(q, k, v)


def _mixout_kernel(x_ref, og_ref, om_ref, ml_ref, mc_ref, gpost_ref, wg_ref, wm_ref, o_ref, *, n_ctx):
    tm = x_ref.shape[1]
    _, _, gate = _mod_rows(ml_ref, mc_ref, 1, _is_ctx_rows(tm, n_ctx))
    y = _dot(og_ref[0], wg_ref[...]) + _dot(om_ref[0], wm_ref[...])
    o_ref[0] = x_ref[0] + gate * _rms(y, gpost_ref[...])


def mixout_call(x, o_gdn, o_mla, mod, gpost, w_gdn, w_mla, *, n_ctx, tm):
    b, t, d = x.shape
    tok = lambda w: pl.BlockSpec((1, tm, w), lambda b, i: (b, i, 0))
    kern = functools.partial(_mixout_kernel, n_ctx=n_ctx)
    return pl.pallas_call(
        kern,
        grid=(b, t // tm),
        in_specs=[tok(d), tok(o_gdn.shape[2]), tok(o_mla.shape[2])] + _mod_specs(b, d)
        + [_const_spec((1, d)), _const_spec(w_gdn.shape), _const_spec(w_mla.shape)],
        out_specs=tok(d),
        out_shape=jax.ShapeDtypeStruct(x.shape, F32),
        compiler_params=_params(("arbitrary", "arbitrary")),
        name="mixout",
    )(x, o_gdn, o_mla, mod, mod, gpost, w_gdn, w_mla)


def _head_pad(w, width):
    kdim = w.shape[0]
    w = w.reshape(kdim, MLA_HEADS, width)
    return jnp.pad(w, ((0, 0), (0, 0), (0, LANES - width))).reshape(kdim, MLA_HEADS * LANES)


def _layout_w_in(w):
    d = w.shape[0]
    off_a = 4 * GDN_WIDTH
    off_b = off_a + 2 * GDN_HEADS
    off_cq = off_b + 2 * GDN_HEADS
    off_ckv = off_cq + Q_RANK
    off_kr = off_ckv + KV_RANK
    small = jnp.concatenate(
        [jnp.zeros((d, ROPE_LANE0), w.dtype), w[:, off_kr:off_kr + MLA_ROPE], w[:, off_a:off_cq],
         jnp.zeros((d, LANES - GATE_LANE0 - 4 * GDN_HEADS), w.dtype)], axis=1)
    return jnp.concatenate([w[:, :off_a], w[:, off_cq:off_kr], small], axis=1).astype(BF16)


def _rope_tables(n_ctx, seq):
    rows = seq // GRID_W
    row = jnp.repeat(jnp.arange(rows), GRID_W).astype(F32)
    col = jnp.tile(jnp.arange(GRID_W), rows).astype(F32)
    axis_dim = MLA_ROPE // 2
    inv_freq = jnp.power(ROPE_BASE, -jnp.arange(0, axis_dim, 2, dtype=F32) / axis_dim)
    ang = jnp.concatenate([row[:, None] * inv_freq, col[:, None] * inv_freq], axis=-1)
    cos, sin = jnp.cos(ang), jnp.sin(ang)
    ones = lambda n: jnp.ones((seq, n), F32)
    zeros = lambda n: jnp.zeros((seq, n), F32)
    tail = LANES - GATE_LANE0
    c = jnp.concatenate([ones(ROPE_LANE0), cos, cos, ones(tail)], axis=1)
    s1 = jnp.concatenate([zeros(ROPE_LANE0 + ROPE_HALF), sin, zeros(tail)], axis=1)
    s2 = jnp.concatenate([zeros(ROPE_LANE0), -sin, zeros(ROPE_HALF + tail)], axis=1)
    ctx = lambda fill: jnp.full((n_ctx, LANES), fill, F32)
    return (jnp.concatenate([ctx(1.0), c]), jnp.concatenate([ctx(0.0), s1]),
            jnp.concatenate([ctx(0.0), s2]))


def kernel(x, c, ctx, c_ctx, w_ada, b_ada, norm_pre, norm_post, ffn_w_gate, ffn_w_up, ffn_w_down,
           w_in, gdn_conv, gdn_a_log, gdn_dt_bias, gdn_out_norm, mla_q_norm, mla_kv_norm, mla_w_uq,
           mla_w_ukv, w_out):
    bsz, seq, d = x.shape
    n_ctx = ctx.shape[1]
    t = n_ctx + seq
    depth = w_ada.shape[0]
    tm = _pick(t, (768, 512, 384, 256, 128))
    tq = _pick(n_ctx, (256, 128))
    ck = tq
    assert seq % ck == 0 and seq % GRID_W == 0

    cond_rows = -(-(bsz + 1) // 8) * 8
    cond = jnp.concatenate([c, c_ctx[None], jnp.zeros((cond_rows - bsz - 1, d), F32)], axis=0)
    mod = ada_call(cond, w_ada, b_ada).reshape(depth, cond_rows, N_MOD, d)
    rope_c, rope_s1, rope_s2 = _rope_tables(n_ctx, seq)

    xs = jnp.concatenate([ctx, x], axis=1)
    for l in range(depth):
        row = lambda v: v.reshape(1, -1)
        ffn = lambda xs, slot, j: ffn_call(
            xs, mod[l], row(norm_pre[l, slot]), row(norm_post[l, slot]),
            ffn_w_gate[l, j].astype(BF16), ffn_w_up[l, j].astype(BF16),
            ffn_w_down[l, j].astype(BF16), slot=slot, n_ctx=n_ctx, tm=tm)
        xs = ffn(xs, 0, 0)

        uq = mla_w_uq[l].reshape(Q_RANK, MLA_HEADS, MLA_NOPE + MLA_ROPE)
        ukv = mla_w_ukv[l].reshape(KV_RANK, MLA_HEADS, MLA_NOPE + MLA_V)
        wuq = _head_pad(uq.reshape(Q_RANK, -1), MLA_NOPE + MLA_ROPE).astype(BF16)
        wuk = _head_pad(ukv[:, :, :MLA_NOPE].reshape(KV_RANK, -1), MLA_NOPE).astype(BF16)
        wuv = ukv[:, :, MLA_NOPE:].reshape(KV_RANK, -1).astype(BF16)
        qkvz, small, q, k, v = mixin_call(
            xs, mod[l], row(norm_pre[l, 1]), _layout_w_in(w_in[l]), row(mla_q_norm[l]),
            row(mla_kv_norm[l]), wuq, wuk, wuv, rope_c, rope_s1, rope_s2, n_ctx=n_ctx, tm=tm)

        ab = small[:, :, GATE_LANE0:GATE_LANE0 + 4 * GDN_HEADS]
        gates = ab.reshape(bsz, t, 4, GDN_HEADS).transpose(0, 3, 1, 2)
        o_gdn = gdn_call(qkvz, gates, gdn_conv[l], gdn_a_log[l], gdn_dt_bias[l],
                         row(gdn_out_norm[l]), n_ctx=n_ctx, ck=ck)
        o_mla = attn_call(q, k, v, n_ctx=n_ctx, tq=tq)
        xs = mixout_call(xs, o_gdn, o_mla, mod[l], row(norm_post[l, 1]),
                         w_out[l, :GDN_WIDTH].astype(BF16), w_out[l, GDN_WIDTH:].astype(BF16),
                         n_ctx=n_ctx, tm=tm)
        xs = ffn(xs, 2, 1)
    return xs[:, n_ctx:]
```

```python
import functools

import jax
import jax.numpy as jnp
from jax import lax
from jax.experimental import pallas as pl
from jax.experimental.pallas import tpu as pltpu

F32 = jnp.float32
BF16 = jnp.bfloat16

EPS = 1e-6
N_MOD = 9
GDN_HEADS = 4
GDN_HEAD_DIM = 128
GDN_WIDTH = GDN_HEADS * GDN_HEAD_DIM
CONV_K = 5
CONV_PAD = CONV_K // 2
MLA_HEADS = 8
MLA_NOPE = 64
MLA_ROPE = 32
MLA_V = 64
Q_RANK = 384
KV_RANK = 256
ROPE_BASE = 10000.0
GRID_W = 64
MLA_SCALE = (MLA_NOPE + MLA_ROPE) ** -0.5

LANES = 128
ROPE_HALF = MLA_ROPE // 2
ROPE_LANE0 = MLA_NOPE
GATE_LANE0 = ROPE_LANE0 + MLA_ROPE
VMEM_LIMIT = 56 * 1024 * 1024


def _pick(n, candidates):
    for cand in candidates:
        if n % cand == 0:
            return cand
    raise ValueError(f"no tile for {n}")


def _sigmoid(x):
    return 1.0 / (1.0 + jnp.exp(-x))


def _silu(x):
    return x * _sigmoid(x)


def _rms(x, gain):
    return x * lax.rsqrt(jnp.mean(x * x, axis=-1, keepdims=True) + EPS) * gain


def _dot(a, b):
    return jnp.dot(a, b, preferred_element_type=F32)


def _dot_nt(a, b):
    return lax.dot_general(a, b, (((1,), (1,)), ((), ())), preferred_element_type=F32)


def _const_spec(shape):
    zeros = (0,) * len(shape)
    return pl.BlockSpec(shape, lambda *_: zeros, pipeline_mode=pl.Buffered(1))


def _params(sem):
    return pltpu.CompilerParams(dimension_semantics=sem, vmem_limit_bytes=VMEM_LIMIT)


def _ada_kernel(s_ref, w_ref, b_ref, o_ref):
    s = _silu(s_ref[...])
    o_ref[0] = jnp.dot(s, w_ref[0], precision=lax.Precision.HIGHEST,
                       preferred_element_type=F32) + b_ref[0]


def ada_call(cond, w_ada, b_ada):
    depth, d, n = w_ada.shape
    rows = cond.shape[0]
    tn = _pick(n, (1024, 512, 256, 128))
    return pl.pallas_call(
        _ada_kernel,
        grid=(depth, n // tn),
        in_specs=[pl.BlockSpec((rows, d), lambda l, j: (0, 0)),
                  pl.BlockSpec((1, d, tn), lambda l, j: (l, 0, j)),
                  pl.BlockSpec((1, 1, tn), lambda l, j: (l, 0, j))],
        out_specs=pl.BlockSpec((1, rows, tn), lambda l, j: (l, 0, j)),
        out_shape=jax.ShapeDtypeStruct((depth, rows, n), F32),
        compiler_params=_params(("arbitrary", "arbitrary")),
        name="ada",
    )(cond, w_ada, b_ada.reshape(depth, 1, n))


def _mod_rows(ml_ref, mc_ref, slot, is_ctx):
    out = []
    for j in range(3):
        lat = ml_ref[0, pl.ds(3 * slot + j, 1), :]
        ctx = mc_ref[0, pl.ds(3 * slot + j, 1), :]
        out.append(jnp.where(is_ctx, ctx, lat))
    return out


def _is_ctx_rows(tm, n_ctx):
    rows = pl.program_id(1) * tm + lax.broadcasted_iota(jnp.int32, (tm, 1), 0)
    return rows < n_ctx


def _mod_specs(n_ctx_row, d):
    return [pl.BlockSpec((1, N_MOD, d), lambda b, i: (b, 0, 0)),
            pl.BlockSpec((1, N_MOD, d), lambda b, i: (n_ctx_row, 0, 0))]


def _ffn_kernel(x_ref, ml_ref, mc_ref, gpre_ref, gpost_ref, wg_ref, wu_ref, wd_ref, o_ref,
                *, slot, n_ctx, fc):
    tm = x_ref.shape[1]
    x = x_ref[0]
    shift, scale, gate = _mod_rows(ml_ref, mc_ref, slot, _is_ctx_rows(tm, n_ctx))
    h = (_rms(x, gpre_ref[...]) * (1.0 + scale) + shift).astype(BF16)
    y = jnp.zeros(x.shape, F32)
    for j in range(wg_ref.shape[1] // fc):
        cols = pl.ds(j * fc, fc)
        act = _silu(_dot(h, wg_ref[:, cols])) * _dot(h, wu_ref[:, cols])
        y = y + _dot(act.astype(BF16), wd_ref[cols, :])
    o_ref[0] = x + 0.5 * gate * _rms(y, gpost_ref[...])


def ffn_call(x, mod, gpre, gpost, wg, wu, wd, *, slot, n_ctx, tm):
    b, t, d = x.shape
    dff = wg.shape[1]
    fc = _pick(dff, (256, 128))
    kern = functools.partial(_ffn_kernel, slot=slot, n_ctx=n_ctx, fc=fc)
    return pl.pallas_call(
        kern,
        grid=(b, t // tm),
        in_specs=[pl.BlockSpec((1, tm, d), lambda b, i: (b, i, 0))]
        + _mod_specs(b, d)
        + [_const_spec((1, d)), _const_spec((1, d)),
           _const_spec((d, dff)), _const_spec((d, dff)), _const_spec((dff, d))],
        out_specs=pl.BlockSpec((1, tm, d), lambda b, i: (b, i, 0)),
        out_shape=jax.ShapeDtypeStruct(x.shape, F32),
        compiler_params=_params(("arbitrary", "arbitrary")),
        name="ffn",
    )(x, mod, mod, gpre, gpost, wg, wu, wd)


def _rope(x, c_ref, s1_ref, s2_ref):
    return (x * c_ref[...] + pltpu.roll(x, ROPE_HALF, 1) * s1_ref[...]
            + pltpu.roll(x, LANES - ROPE_HALF, 1) * s2_ref[...])


def _mixin_kernel(x_ref, ml_ref, mc_ref, gpre_ref, win_ref, qn_ref, kvn_ref, wuq_ref, wuk_ref,
                  wuv_ref, gpar_ref, c_ref, s1_ref, s2_ref,
                  qkvz_ref, small_ref, q_ref, k_ref, v_ref, *, n_ctx):
    tm = x_ref.shape[1]
    shift, scale, _ = _mod_rows(ml_ref, mc_ref, 1, _is_ctx_rows(tm, n_ctx))
    h = (_rms(x_ref[0], gpre_ref[...]) * (1.0 + scale) + shift).astype(BF16)
    n_gdn = qkvz_ref.shape[2]
    qkvz_ref[0] = _dot(h, win_ref[:, pl.ds(0, n_gdn)])
    rest = _dot(h, win_ref[:, pl.ds(n_gdn, Q_RANK + KV_RANK + LANES)])
    c_q = rest[:, :Q_RANK]
    c_kv = rest[:, Q_RANK:Q_RANK + KV_RANK]
    small = _rope(rest[:, Q_RANK + KV_RANK:], c_ref, s1_ref, s2_ref)
    lane = lax.broadcasted_iota(jnp.int32, (1, LANES), 1)
    pre = small + gpar_ref[pl.ds(1, 1), :]
    softplus = jnp.maximum(pre, 0.0) + jnp.log(1.0 + jnp.exp(-jnp.abs(pre)))
    log_decay = -jnp.exp(gpar_ref[pl.ds(0, 1), :]) * softplus
    n_dec = 2 * GDN_HEADS
    is_dec = (lane >= GATE_LANE0) & (lane < GATE_LANE0 + n_dec)
    is_beta = (lane >= GATE_LANE0 + n_dec) & (lane < GATE_LANE0 + 2 * n_dec)
    small_ref[0] = jnp.where(is_dec, log_decay, jnp.where(is_beta, _sigmoid(small), small))
    k_rope = jnp.where((lane >= ROPE_LANE0) & (lane < GATE_LANE0), small, 0.0)

    q = _dot(_rms(c_q, qn_ref[...]).astype(BF16), wuq_ref[...]) * MLA_SCALE
    ckv_n = _rms(c_kv, kvn_ref[...]).astype(BF16)
    k = _dot(ckv_n, wuk_ref[...])
    for hd in range(MLA_HEADS):
        cols = pl.ds(hd * LANES, LANES)
        blk = slice(hd * LANES, (hd + 1) * LANES)
        q_ref[0, :, cols] = _rope(q[:, blk], c_ref, s1_ref, s2_ref).astype(BF16)
        k_ref[0, :, cols] = (k[:, blk] + k_rope).astype(BF16)
    v_ref[0] = _dot(ckv_n, wuv_ref[...]).astype(BF16)


def mixin_call(x, mod, gpre, win, qn, kvn, wuq, wuk, wuv, gpar, rope_c, rope_s1, rope_s2, *, n_ctx,
               tm):
    b, t, d = x.shape
    n_gdn = 4 * GDN_WIDTH
    hw = MLA_HEADS * LANES
    vw = MLA_HEADS * MLA_V
    tok = lambda w: pl.BlockSpec((1, tm, w), lambda b, i: (b, i, 0))
    tab = pl.BlockSpec((tm, LANES), lambda b, i: (i, 0))
    kern = functools.partial(_mixin_kernel, n_ctx=n_ctx)
    return pl.pallas_call(
        kern,
        grid=(b, t // tm),
        in_specs=[tok(d)] + _mod_specs(b, d)
        + [_const_spec((1, d)), _const_spec(win.shape), _const_spec((1, Q_RANK)),
           _const_spec((1, KV_RANK)), _const_spec(wuq.shape), _const_spec(wuk.shape),
           _const_spec(wuv.shape), _const_spec(gpar.shape), tab, tab, tab],
        out_specs=[tok(n_gdn), tok(LANES), tok(hw), tok(hw), tok(vw)],
        out_shape=[jax.ShapeDtypeStruct((b, t, n_gdn), F32),
                   jax.ShapeDtypeStruct((b, t, LANES), F32),
                   jax.ShapeDtypeStruct((b, t, hw), BF16),
                   jax.ShapeDtypeStruct((b, t, hw), BF16),
                   jax.ShapeDtypeStruct((b, t, vw), BF16)],
        compiler_params=_params(("arbitrary", "arbitrary")),
        name="mixin",
    )(x, mod, mod, gpre, win, qn, kvn, wuq, wuk, wuv, gpar, rope_c, rope_s1, rope_s2)


def _level_masks(ck):
    ii = lax.broadcasted_iota(jnp.int32, (ck, ck), 0)
    jj = lax.broadcasted_iota(jnp.int32, (ck, ck), 1)
    masks = [ii == jj]
    for lvl in range(1, ck.bit_length()):
        masks.append(((ii >> lvl) == (jj >> lvl)) & ((ii >> (lvl - 1)) != (jj >> (lvl - 1))))
    return jnp.stack(masks).astype(BF16)


def _gdn_kernel(q_ref, k_ref, v_ref, z_ref, cq_ref, ck_ref, cv_ref, gate_ref, lvl_ref, onorm_ref,
                o_ref, q_s, k_s, v_s, gb_s, u0_s, w_s, qd_s, kdt_s, qk_s, of_s, ob_s,
                *, n_ctx, ck, unroll):
    t = q_ref.shape[1]
    dk = q_ref.shape[2]
    nc = t // ck
    nl = n_ctx // ck
    n_lvl = lvl_ref.shape[0] - 1

    row = lax.broadcasted_iota(jnp.int32, (t, 1), 0)

    def conv(u_ref, w_ref):
        u = u_ref[0]
        acc = u * w_ref[pl.ds(CONV_PAD, 1), :]
        for s in range(-CONV_PAD, CONV_PAD + 1):
            if s == 0:
                continue
            src = row + s
            valid = (src >= 0) & (src < t) & ((src >= n_ctx) == (row >= n_ctx))
            shifted = pltpu.roll(u, (-s) % t, 0)
            acc = acc + jnp.where(valid, shifted, 0.0) * w_ref[pl.ds(CONV_PAD + s, 1), :]
        return _silu(acc)

    def l2n(u):
        return u * lax.rsqrt(jnp.sum(u * u, axis=-1, keepdims=True) + EPS)

    qf = l2n(conv(q_ref, cq_ref)) * (dk ** -0.5)
    kf = l2n(conv(k_ref, ck_ref))
    vf = conv(v_ref, cv_ref)
    for c in range(nc):
        rows = slice(c * ck, (c + 1) * ck)
        q_s[c] = qf[rows]
        k_s[c] = kf[rows]
        v_s[c] = vf[rows]
        gb_s[c] = gate_ref[0, 0, rows, :]

    ii = lax.broadcasted_iota(jnp.int32, (ck, ck), 0)
    jj = lax.broadcasted_iota(jnp.int32, (ck, ck), 1)
    eye = ii == jj

    def group_body(gi, carry):
        chains = []
        for u in range(unroll):
            c = gi * unroll + u
            q = q_s[c]
            k = k_s[c]
            v = v_s[c]
            gb = gb_s[c]
            kb = k.astype(BF16)
            qk_raw = _dot_nt(q.astype(BF16), kb)
            k_t = k.T
            for d in range(2):
                incl = (ii >= jj) if d == 0 else (ii <= jj)
                strict = (ii > jj) if d == 0 else (ii < jj)
                incl_t = (ii <= jj) if d == 0 else (ii >= jj)
                g_col = gb[:, d:d + 1]
                beta = gb[:, 2 + d:3 + d]
                g_row = jnp.sum(jnp.where(eye, g_col, 0.0), axis=0, keepdims=True)
                cum_col = jnp.sum(jnp.where(incl, g_row, 0.0), axis=1, keepdims=True)
                cum_row = jnp.sum(jnp.where(incl_t, g_col, 0.0), axis=0, keepdims=True)
                total = jnp.sum(g_col, axis=0, keepdims=True)
                e = jnp.exp(cum_col - cum_row)
                kk = _dot_nt((beta * k).astype(BF16), kb)
                a = (jnp.where(strict, e, 0.0) * kk).astype(BF16)
                qk_s[d, c] = (jnp.where(incl, e, 0.0) * qk_raw).astype(BF16)
                e_cum = jnp.exp(cum_col)
                rhs = jnp.concatenate([beta * v, (beta * e_cum) * k], axis=1).astype(BF16)
                qd_s[d, c] = (q * e_cum).astype(BF16)
                kdt_s[d, c] = (k_t * jnp.exp(total - cum_row)).astype(BF16)
                chains.append([d, c, a, rhs, lvl_ref[0] - a * lvl_ref[1]])
        for lvl in range(2, n_lvl + 1):
            xa = [_dot(ch[4], ch[2]).astype(BF16) for ch in chains]
            ys = [_dot(m, ch[4]) for m, ch in zip(xa, chains)]
            for ch, y in zip(chains, ys):
                ch[4] = ch[4] - y.astype(BF16) * lvl_ref[lvl]
        for d, c, _, rhs, x in chains:
            sol = _dot(x, rhs)
            u0_s[d, c] = sol[:, :dk]
            w_s[d, c] = sol[:, dk:].astype(BF16)
        return carry

    lax.fori_loop(0, nc // unroll, group_body, 0)

    def scan_body(i, carry):
        s_f, s_b = carry
        c_f = i
        c_b = jnp.where(i < nl, nl - 1 - i, nc - 1 + nl - i)
        new = []
        for d, c, s, o_s in ((0, c_f, s_f, of_s), (1, c_b, s_b, ob_s)):
            sb = s.astype(BF16)
            u = u0_s[d, c] - _dot(w_s[d, c], sb)
            ub = u.astype(BF16)
            o_s[c] = _dot(qd_s[d, c], sb) + _dot(qk_s[d, c], ub)
            total = jnp.sum(gb_s[c][:, d:d + 1], axis=0, keepdims=True)
            new.append(jnp.exp(total) * s + _dot(kdt_s[d, c], ub))
        return tuple(new)

    zero = jnp.zeros((dk, dk), F32)
    lax.fori_loop(0, nc, scan_body, (zero, zero))

    for c in range(nc):
        o = _rms(of_s[c] + ob_s[c], onorm_ref[...])
        o_ref[0, pl.ds(c * ck, ck), :] = (o * _silu(z_ref[0, pl.ds(c * ck, ck), :])).astype(BF16)


def gdn_call(qkvz, gates, conv_w, out_norm, *, n_ctx, ck):
    b, t, _ = qkvz.shape
    h, dk = GDN_HEADS, GDN_HEAD_DIM
    nc = t // ck
    unroll = _pick(nc, (3, 2, 1))
    lvl = _level_masks(ck)
    col = lambda off: pl.BlockSpec((1, t, dk), lambda b, hd: (b, 0, off + hd))
    cw = lambda off: pl.BlockSpec((CONV_K, dk), lambda b, hd: (0, off + hd))
    chunked = lambda w: pltpu.VMEM((nc, ck, w), F32)
    both = lambda r, w, dt: pltpu.VMEM((2, nc, r, w), dt)
    kern = functools.partial(_gdn_kernel, n_ctx=n_ctx, ck=ck, unroll=unroll)
    return pl.pallas_call(
        kern,
        grid=(b, h),
        in_specs=[col(0), col(h), col(2 * h), col(3 * h), cw(0), cw(h), cw(2 * h),
                  pl.BlockSpec((1, 1, t, 4), lambda b, hd: (b, hd, 0, 0)),
                  _const_spec(lvl.shape),
                  pl.BlockSpec((1, dk), lambda b, hd: (0, 0))],
        out_specs=pl.BlockSpec((1, t, dk), lambda b, hd: (b, 0, hd)),
        out_shape=jax.ShapeDtypeStruct((b, t, h * dk), BF16),
        scratch_shapes=[chunked(dk), chunked(dk), chunked(dk), chunked(4),
                        both(ck, dk, F32), both(ck, dk, BF16), both(ck, dk, BF16),
                        both(dk, ck, BF16), both(ck, ck, BF16),
                        chunked(dk), chunked(dk)],
        compiler_params=_params(("arbitrary", "arbitrary")),
        name="gdn",
    )(qkvz, qkvz, qkvz, qkvz, conv_w, conv_w, conv_w, gates, lvl, out_norm)


def _attn_kernel(q_ref, k_ref, v_ref, o_ref, *, n_ctx):
    tq = q_ref.shape[1]
    t = k_ref.shape[1]
    group = 2 * LANES // MLA_V
    lane = lax.broadcasted_iota(jnp.int32, (1, group * MLA_V), 1)

    def attend(n_keys):
        for g0 in range(0, MLA_HEADS, group):
            vg = v_ref[0, pl.ds(0, n_keys), pl.ds(g0 * MLA_V, group * MLA_V)]
            out = jnp.zeros((tq, group * MLA_V), F32)
            for hd in range(g0, g0 + group):
                cols = pl.ds(hd * LANES, LANES)
                s = _dot_nt(q_ref[0, :, cols], k_ref[0, pl.ds(0, n_keys), cols])
                p = jnp.exp(s - jnp.max(s, axis=-1, keepdims=True))
                denom = jnp.sum(p, axis=-1, keepdims=True)
                pv = _dot(p.astype(BF16), vg) / denom
                j = hd - g0
                out = jnp.where((lane >= j * MLA_V) & (lane < (j + 1) * MLA_V), pv, out)
            o_ref[0, :, pl.ds(g0 * MLA_V, group * MLA_V)] = out.astype(BF16)

    is_ctx = pl.program_id(1) * tq < n_ctx

    @pl.when(is_ctx)
    def _():
        attend(n_ctx)

    @pl.when(jnp.logical_not(is_ctx))
    def _():
        attend(t)


def attn_call(q, k, v, *, n_ctx, tq):
    b, t, hw = q.shape
    vw = v.shape[2]
    kern = functools.partial(_attn_kernel, n_ctx=n_ctx)
    return pl.pallas_call(
        kern,
        grid=(b, t // tq),
        in_specs=[pl.BlockSpec((1, tq, hw), lambda b, i: (b, i, 0)),
                  pl.BlockSpec((1, t, hw), lambda b, i: (b, 0, 0)),
                  pl.BlockSpec((1, t, vw), lambda b, i: (b, 0, 0))],
        out_specs=pl.BlockSpec((1, tq, vw), lambda b, i: (b, i, 0)),
        out_shape=jax.ShapeDtypeStruct((b, t, vw), BF16),
        compiler_params=_params(("arbitrary", "arbitrary")),
        name="attn",
    )(q, k, v)


def _mixout_kernel(x_ref, og_ref, om_ref, ml_ref, mc_ref, gpost_ref, wg_ref, wm_ref, o_ref, *, n_ctx):
    tm = x_ref.shape[1]
    _, _, gate = _mod_rows(ml_ref, mc_ref, 1, _is_ctx_rows(tm, n_ctx))
    y = _dot(og_ref[0], wg_ref[...]) + _dot(om_ref[0], wm_ref[...])
    o_ref[0] = x_ref[0] + gate * _rms(y, gpost_ref[...])


def mixout_call(x, o_gdn, o_mla, mod, gpost, w_gdn, w_mla, *, n_ctx, tm):
    b, t, d = x.shape
    tok = lambda w: pl.BlockSpec((1, tm, w), lambda b, i: (b, i, 0))
    kern = functools.partial(_mixout_kernel, n_ctx=n_ctx)
    return pl.pallas_call(
        kern,
        grid=(b, t // tm),
        in_specs=[tok(d), tok(o_gdn.shape[2]), tok(o_mla.shape[2])] + _mod_specs(b, d)
        + [_const_spec((1, d)), _const_spec(w_gdn.shape), _const_spec(w_mla.shape)],
        out_specs=tok(d),
        out_shape=jax.ShapeDtypeStruct(x.shape, F32),
        compiler_params=_params(("arbitrary", "arbitrary")),
        name="mixout",
    )(x, o_gdn, o_mla, mod, mod, gpost, w_gdn, w_mla)


def _head_pad(w, width):
    kdim = w.shape[0]
    w = w.reshape(kdim, MLA_HEADS, width)
    return jnp.pad(w, ((0, 0), (0, 0), (0, LANES - width))).reshape(kdim, MLA_HEADS * LANES)


def _layout_w_in(w):
    d = w.shape[0]
    off_a = 4 * GDN_WIDTH
    off_b = off_a + 2 * GDN_HEADS
    off_cq = off_b + 2 * GDN_HEADS
    off_ckv = off_cq + Q_RANK
    off_kr = off_ckv + KV_RANK
    small = jnp.concatenate(
        [jnp.zeros((d, ROPE_LANE0), w.dtype), w[:, off_kr:off_kr + MLA_ROPE], w[:, off_a:off_cq],
         jnp.zeros((d, LANES - GATE_LANE0 - 4 * GDN_HEADS), w.dtype)], axis=1)
    return jnp.concatenate([w[:, :off_a], w[:, off_cq:off_kr], small], axis=1).astype(BF16)


def _rope_tables(n_ctx, seq):
    rows = seq // GRID_W
    row = jnp.repeat(jnp.arange(rows), GRID_W).astype(F32)
    col = jnp.tile(jnp.arange(GRID_W), rows).astype(F32)
    axis_dim = MLA_ROPE // 2
    inv_freq = jnp.power(ROPE_BASE, -jnp.arange(0, axis_dim, 2, dtype=F32) / axis_dim)
    ang = jnp.concatenate([row[:, None] * inv_freq, col[:, None] * inv_freq], axis=-1)
    cos, sin = jnp.cos(ang), jnp.sin(ang)
    ones = lambda n: jnp.ones((seq, n), F32)
    zeros = lambda n: jnp.zeros((seq, n), F32)
    tail = LANES - GATE_LANE0
    c = jnp.concatenate([ones(ROPE_LANE0), cos, cos, ones(tail)], axis=1)
    s1 = jnp.concatenate([zeros(ROPE_LANE0 + ROPE_HALF), sin, zeros(tail)], axis=1)
    s2 = jnp.concatenate([zeros(ROPE_LANE0), -sin, zeros(ROPE_HALF + tail)], axis=1)
    ctx = lambda fill: jnp.full((n_ctx, LANES), fill, F32)
    return (jnp.concatenate([ctx(1.0), c]), jnp.concatenate([ctx(0.0), s1]),
            jnp.concatenate([ctx(0.0), s2]))


def kernel(x, c, ctx, c_ctx, w_ada, b_ada, norm_pre, norm_post, ffn_w_gate, ffn_w_up, ffn_w_down,
           w_in, gdn_conv, gdn_a_log, gdn_dt_bias, gdn_out_norm, mla_q_norm, mla_kv_norm, mla_w_uq,
           mla_w_ukv, w_out):
    bsz, seq, d = x.shape
    n_ctx = ctx.shape[1]
    t = n_ctx + seq
    depth = w_ada.shape[0]
    tm = _pick(t, (768, 512, 384, 256, 128))
    tq = _pick(n_ctx, (256, 128))
    ck = tq
    assert seq % ck == 0 and seq % GRID_W == 0

    cond_rows = -(-(bsz + 1) // 8) * 8
    cond = jnp.concatenate([c, c_ctx[None], jnp.zeros((cond_rows - bsz - 1, d), F32)], axis=0)
    mod = ada_call(cond, w_ada, b_ada).reshape(depth, cond_rows, N_MOD, d)
    rope_c, rope_s1, rope_s2 = _rope_tables(n_ctx, seq)

    xs = jnp.concatenate([ctx, x], axis=1)
    for l in range(depth):
        row = lambda v: v.reshape(1, -1)
        ffn = lambda xs, slot, j: ffn_call(
            xs, mod[l], row(norm_pre[l, slot]), row(norm_post[l, slot]),
            ffn_w_gate[l, j].astype(BF16), ffn_w_up[l, j].astype(BF16),
            ffn_w_down[l, j].astype(BF16), slot=slot, n_ctx=n_ctx, tm=tm)
        xs = ffn(xs, 0, 0)

        uq = mla_w_uq[l].reshape(Q_RANK, MLA_HEADS, MLA_NOPE + MLA_ROPE)
        ukv = mla_w_ukv[l].reshape(KV_RANK, MLA_HEADS, MLA_NOPE + MLA_V)
        wuq = _head_pad(uq.reshape(Q_RANK, -1), MLA_NOPE + MLA_ROPE).astype(BF16)
        wuk = _head_pad(ukv[:, :, :MLA_NOPE].reshape(KV_RANK, -1), MLA_NOPE).astype(BF16)
        wuv = ukv[:, :, MLA_NOPE:].reshape(KV_RANK, -1).astype(BF16)
        gpar = jnp.pad(jnp.stack([gdn_a_log[l].reshape(-1), gdn_dt_bias[l].reshape(-1)]),
                       ((0, 0), (GATE_LANE0, LANES - GATE_LANE0 - 2 * GDN_HEADS)))
        qkvz, small, q, k, v = mixin_call(
            xs, mod[l], row(norm_pre[l, 1]), _layout_w_in(w_in[l]), row(mla_q_norm[l]),
            row(mla_kv_norm[l]), wuq, wuk, wuv, gpar, rope_c, rope_s1, rope_s2, n_ctx=n_ctx, tm=tm)

        ab = small[:, :, GATE_LANE0:GATE_LANE0 + 4 * GDN_HEADS]
        gates = ab.reshape(bsz, t, 4, GDN_HEADS).transpose(0, 3, 1, 2)
        o_gdn = gdn_call(qkvz, gates, gdn_conv[l], row(gdn_out_norm[l]), n_ctx=n_ctx, ck=ck)
        o_mla = attn_call(q, k, v, n_ctx=n_ctx, tq=tq)
        xs = mixout_call(xs, o_gdn, o_mla, mod[l], row(norm_post[l, 1]),
                         w_out[l, :GDN_WIDTH].astype(BF16), w_out[l, GDN_WIDTH:].astype(BF16),
                         n_ctx=n_ctx, tm=tm)
        xs = ffn(xs, 2, 1)
    return xs[:, n_ctx:]
```

```python
import functools
import math

import jax
import jax.numpy as jnp
from jax import lax
from jax.experimental import pallas as pl
from jax.experimental.pallas import tpu as pltpu

F32 = jnp.float32
BF16 = jnp.bfloat16

EPS = 1e-6
N_MOD = 9
GDN_HEADS = 4
GDN_HEAD_DIM = 128
GDN_WIDTH = GDN_HEADS * GDN_HEAD_DIM
CONV_K = 5
CONV_PAD = CONV_K // 2
MLA_HEADS = 8
MLA_NOPE = 64
MLA_ROPE = 32
MLA_V = 64
Q_RANK = 384
KV_RANK = 256
ROPE_BASE = 10000.0
GRID_W = 64
MLA_SCALE = (MLA_NOPE + MLA_ROPE) ** -0.5
LOG2_E = math.log2(math.e)

LANES = 128
SUBLANES = 8
ROPE_HALF = MLA_ROPE // 2
ROPE_LANE0 = MLA_NOPE
GATE_LANE0 = ROPE_LANE0 + MLA_ROPE
HEAD_GROUP = 4
VMEM_LIMIT = 56 * 1024 * 1024


def _pick(n, candidates):
    for cand in candidates:
        if n % cand == 0:
            return cand
    raise ValueError(f"no tile for {n}")


def _sigmoid(x):
    return 1.0 / (1.0 + jnp.exp(-x))


def _silu(x):
    return x * _sigmoid(x)


def _rms(x, gain):
    return x * lax.rsqrt(jnp.mean(x * x, axis=-1, keepdims=True) + EPS) * gain


def _dot(a, b):
    return jnp.dot(a, b, preferred_element_type=F32)


def _dot_nt(a, b):
    return lax.dot_general(a, b, (((1,), (1,)), ((), ())), preferred_element_type=F32)


def _const_spec(shape):
    zeros = (0,) * len(shape)
    return pl.BlockSpec(shape, lambda *_: zeros, pipeline_mode=pl.Buffered(1))


def _params(sem):
    return pltpu.CompilerParams(dimension_semantics=sem, vmem_limit_bytes=VMEM_LIMIT)


def _ada_kernel(s_ref, w_ref, b_ref, o_ref):
    s = _silu(s_ref[...])
    o_ref[0] = jnp.dot(s, w_ref[0], precision=lax.Precision.HIGHEST,
                       preferred_element_type=F32) + b_ref[0]


def ada_call(cond, w_ada, b_ada):
    depth, d, n = w_ada.shape
    rows = cond.shape[0]
    tn = _pick(n, (1024, 512, 256, 128))
    return pl.pallas_call(
        _ada_kernel,
        grid=(depth, n // tn),
        in_specs=[pl.BlockSpec((rows, d), lambda l, j: (0, 0)),
                  pl.BlockSpec((1, d, tn), lambda l, j: (l, 0, j)),
                  pl.BlockSpec((1, 1, tn), lambda l, j: (l, 0, j))],
        out_specs=pl.BlockSpec((1, rows, tn), lambda l, j: (l, 0, j)),
        out_shape=jax.ShapeDtypeStruct((depth, rows, n), F32),
        compiler_params=_params(("arbitrary", "arbitrary")),
        name="ada",
    )(cond, w_ada, b_ada.reshape(depth, 1, n))


def _mod_rows(ml_ref, mc_ref, slot, is_ctx):
    out = []
    for j in range(3):
        lat = ml_ref[0, pl.ds(3 * slot + j, 1), :]
        ctx = mc_ref[0, pl.ds(3 * slot + j, 1), :]
        out.append(jnp.where(is_ctx, ctx, lat))
    return out


def _is_ctx_rows(tm, n_ctx):
    rows = pl.program_id(1) * tm + lax.broadcasted_iota(jnp.int32, (tm, 1), 0)
    return rows < n_ctx


def _mod_specs(n_ctx_row, d):
    return [pl.BlockSpec((1, N_MOD, d), lambda b, i: (b, 0, 0)),
            pl.BlockSpec((1, N_MOD, d), lambda b, i: (n_ctx_row, 0, 0))]


def _ffn_tile(x, is_ctx, ml_ref, mc_ref, gpre_ref, gpost_ref, wg_ref, wu_ref, wd_ref, slot, fc):
    shift, scale, gate = _mod_rows(ml_ref, mc_ref, slot, is_ctx)
    h = (_rms(x, gpre_ref[...]) * (1.0 + scale) + shift).astype(BF16)
    y = jnp.zeros(x.shape, F32)
    for j in range(wg_ref.shape[1] // fc):
        cols = pl.ds(j * fc, fc)
        act = _silu(_dot(h, wg_ref[:, cols])) * _dot(h, wu_ref[:, cols])
        y = y + _dot(act.astype(BF16), wd_ref[cols, :])
    return x + 0.5 * gate * _rms(y, gpost_ref[...])


def _ffn_kernel(x_ref, ml_ref, mc_ref, gpre_ref, gpost_ref, wg_ref, wu_ref, wd_ref, o_ref,
                *, slot, n_ctx, fc):
    is_ctx = _is_ctx_rows(x_ref.shape[1], n_ctx)
    o_ref[0] = _ffn_tile(x_ref[0], is_ctx, ml_ref, mc_ref, gpre_ref, gpost_ref, wg_ref, wu_ref,
                         wd_ref, slot, fc)


def _ffn_weight_specs(d, dff):
    return [_const_spec((1, d)), _const_spec((1, d)),
            _const_spec((d, dff)), _const_spec((d, dff)), _const_spec((dff, d))]


def ffn_call(x, mod, gpre, gpost, wg, wu, wd, *, slot, n_ctx, tm):
    b, t, d = x.shape
    dff = wg.shape[1]
    kern = functools.partial(_ffn_kernel, slot=slot, n_ctx=n_ctx, fc=_pick(dff, (256, 128)))
    tok = pl.BlockSpec((1, tm, d), lambda b, i: (b, i, 0))
    return pl.pallas_call(
        kern,
        grid=(b, t // tm),
        in_specs=[tok] + _mod_specs(b, d) + _ffn_weight_specs(d, dff),
        out_specs=tok,
        out_shape=jax.ShapeDtypeStruct(x.shape, F32),
        compiler_params=_params(("arbitrary", "arbitrary")),
        name="ffn",
    )(x, mod, mod, gpre, gpost, wg, wu, wd)


def _mixffn_kernel(x_ref, og_ref, om_ref, ml_ref, mc_ref, gmix_ref, wog_ref, wom_ref,
                   gpre_ref, gpost_ref, wg_ref, wu_ref, wd_ref, o_ref, *, n_ctx, fc):
    is_ctx = _is_ctx_rows(x_ref.shape[1], n_ctx)
    _, _, gate = _mod_rows(ml_ref, mc_ref, 1, is_ctx)
    y = _dot(og_ref[0], wog_ref[...]) + _dot(om_ref[0], wom_ref[...])
    x = x_ref[0] + gate * _rms(y, gmix_ref[...])
    o_ref[0] = _ffn_tile(x, is_ctx, ml_ref, mc_ref, gpre_ref, gpost_ref, wg_ref, wu_ref, wd_ref, 2,
                         fc)


def mixffn_call(x, o_gdn, o_mla, mod, gmix, w_gdn, w_mla, gpre, gpost, wg, wu, wd, *, n_ctx, tm):
    b, t, d = x.shape
    dff = wg.shape[1]
    tok = lambda w: pl.BlockSpec((1, tm, w), lambda b, i: (b, i, 0))
    kern = functools.partial(_mixffn_kernel, n_ctx=n_ctx, fc=_pick(dff, (256, 128)))
    return pl.pallas_call(
        kern,
        grid=(b, t // tm),
        in_specs=[tok(d), tok(o_gdn.shape[2]), tok(o_mla.shape[2])] + _mod_specs(b, d)
        + [_const_spec((1, d)), _const_spec(w_gdn.shape), _const_spec(w_mla.shape)]
        + _ffn_weight_specs(d, dff),
        out_specs=tok(d),
        out_shape=jax.ShapeDtypeStruct(x.shape, F32),
        compiler_params=_params(("arbitrary", "arbitrary")),
        name="mixffn",
    )(x, o_gdn, o_mla, mod, mod, gmix, w_gdn, w_mla, gpre, gpost, wg, wu, wd)


def _rope(x, c_ref, s1_ref, s2_ref):
    return (x * c_ref[...] + pltpu.roll(x, ROPE_HALF, 1) * s1_ref[...]
            + pltpu.roll(x, LANES - ROPE_HALF, 1) * s2_ref[...])


def _mixin_kernel(x_ref, ml_ref, mc_ref, gpre_ref, win_ref, qn_ref, kvn_ref, wuq_ref, wuk_ref,
                  wuv_ref, gpar_ref, c_ref, s1_ref, s2_ref,
                  qkvz_ref, small_ref, q_ref, k_ref, v_ref, *, n_ctx):
    tm = x_ref.shape[1]
    shift, scale, _ = _mod_rows(ml_ref, mc_ref, 1, _is_ctx_rows(tm, n_ctx))
    h = (_rms(x_ref[0], gpre_ref[...]) * (1.0 + scale) + shift).astype(BF16)
    n_gdn = qkvz_ref.shape[2]
    qkvz_ref[0] = _dot(h, win_ref[:, pl.ds(0, n_gdn)])
    rest = _dot(h, win_ref[:, pl.ds(n_gdn, Q_RANK + KV_RANK + LANES)])
    c_q = rest[:, :Q_RANK]
    c_kv = rest[:, Q_RANK:Q_RANK + KV_RANK]
    small = _rope(rest[:, Q_RANK + KV_RANK:], c_ref, s1_ref, s2_ref)
    lane = lax.broadcasted_iota(jnp.int32, (1, LANES), 1)
    pre = small + gpar_ref[pl.ds(1, 1), :]
    softplus = jnp.maximum(pre, 0.0) + jnp.log(1.0 + jnp.exp(-jnp.abs(pre)))
    log_decay = -jnp.exp(gpar_ref[pl.ds(0, 1), :]) * softplus
    n_dec = 2 * GDN_HEADS
    is_dec = (lane >= GATE_LANE0) & (lane < GATE_LANE0 + n_dec)
    is_beta = (lane >= GATE_LANE0 + n_dec) & (lane < GATE_LANE0 + 2 * n_dec)
    small_ref[0] = jnp.where(is_dec, log_decay, jnp.where(is_beta, _sigmoid(small), small))
    k_rope = jnp.where((lane >= ROPE_LANE0) & (lane < GATE_LANE0), small, 0.0)

    q = _dot(_rms(c_q, qn_ref[...]).astype(BF16), wuq_ref[...]) * (MLA_SCALE * LOG2_E)
    ckv_n = _rms(c_kv, kvn_ref[...]).astype(BF16)
    k = _dot(ckv_n, wuk_ref[...])
    for hd in range(MLA_HEADS):
        cols = pl.ds(hd * LANES, LANES)
        blk = slice(hd * LANES, (hd + 1) * LANES)
        q_ref[0, :, cols] = _rope(q[:, blk], c_ref, s1_ref, s2_ref).astype(BF16)
        k_ref[0, :, cols] = (k[:, blk] + k_rope).astype(BF16)
    v_ref[0] = _dot(ckv_n, wuv_ref[...]).astype(BF16)


def mixin_call(x, mod, gpre, win, qn, kvn, wuq, wuk, wuv, gpar, rope_c, rope_s1, rope_s2, *, n_ctx,
               tm):
    b, t, d = x.shape
    n_gdn = 4 * GDN_WIDTH
    hw = MLA_HEADS * LANES
    vw = MLA_HEADS * MLA_V
    tok = lambda w: pl.BlockSpec((1, tm, w), lambda b, i: (b, i, 0))
    tab = pl.BlockSpec((tm, LANES), lambda b, i: (i, 0))
    kern = functools.partial(_mixin_kernel, n_ctx=n_ctx)
    return pl.pallas_call(
        kern,
        grid=(b, t // tm),
        in_specs=[tok(d)] + _mod_specs(b, d)
        + [_const_spec((1, d)), _const_spec(win.shape), _const_spec((1, Q_RANK)),
           _const_spec((1, KV_RANK)), _const_spec(wuq.shape), _const_spec(wuk.shape),
           _const_spec(wuv.shape), _const_spec(gpar.shape), tab, tab, tab],
        out_specs=[tok(n_gdn), tok(LANES), tok(hw), tok(hw), tok(vw)],
        out_shape=[jax.ShapeDtypeStruct((b, t, n_gdn), F32),
                   jax.ShapeDtypeStruct((b, t, LANES), F32),
                   jax.ShapeDtypeStruct((b, t, hw), BF16),
                   jax.ShapeDtypeStruct((b, t, hw), BF16),
                   jax.ShapeDtypeStruct((b, t, vw), BF16)],
        compiler_params=_params(("arbitrary", "arbitrary")),
        name="mixin",
    )(x, mod, mod, gpre, win, qn, kvn, wuq, wuk, wuv, gpar, rope_c, rope_s1, rope_s2)


def _level_masks(ck):
    ii = lax.broadcasted_iota(jnp.int32, (ck, ck), 0)
    jj = lax.broadcasted_iota(jnp.int32, (ck, ck), 1)
    masks = [ii == jj]
    for lvl in range(1, ck.bit_length()):
        masks.append(((ii >> lvl) == (jj >> lvl)) & ((ii >> (lvl - 1)) != (jj >> (lvl - 1))))
    return jnp.stack(masks).astype(BF16)


def _gdn_kernel(q_ref, k_ref, v_ref, z_ref, cq_ref, ck_ref, cv_ref, gate_ref, lvl_ref, onorm_ref,
                o_ref, q_s, k_s, v_s, u0_s, w_s, qd_s, qk_s, n_s, m_s, dec_s, st_s,
                *, n_ctx, ck, unroll):
    t = q_ref.shape[1]
    dk = q_ref.shape[2]
    nc = t // ck
    nl = n_ctx // ck
    n_lvl = lvl_ref.shape[0] - 1
    halo = SUBLANES

    ii = lax.broadcasted_iota(jnp.int32, (ck, ck), 0)
    jj = lax.broadcasted_iota(jnp.int32, (ck, ck), 1)
    eye = ii == jj

    def conv(u_ref, w_ref, c):
        r0 = pl.multiple_of(c * ck, ck)
        lo = pl.multiple_of(jnp.maximum(r0 - halo, 0), halo)
        hi = pl.multiple_of(jnp.minimum(r0 + ck, t - halo), halo)
        keep_lo = jnp.where((c == 0) | (c == nl), 0.0, 1.0)
        keep_hi = jnp.where((c == nl - 1) | (c == nc - 1), 0.0, 1.0)
        full = jnp.concatenate([u_ref[0, pl.ds(lo, halo), :] * keep_lo, u_ref[0, pl.ds(r0, ck), :],
                                u_ref[0, pl.ds(hi, halo), :] * keep_hi], axis=0)
        acc = full[halo:halo + ck] * w_ref[pl.ds(CONV_PAD, 1), :]
        for s in range(-CONV_PAD, CONV_PAD + 1):
            if s != 0:
                acc = acc + full[halo + s:halo + s + ck] * w_ref[pl.ds(CONV_PAD + s, 1), :]
        return _silu(acc)

    def l2n(u):
        return u * lax.rsqrt(jnp.sum(u * u, axis=-1, keepdims=True) + EPS)

    def prep(c):
        q_s[c] = l2n(conv(q_ref, cq_ref, c)) * (dk ** -0.5)
        k_s[c] = l2n(conv(k_ref, ck_ref, c))
        v_s[c] = conv(v_ref, cv_ref, c)

    def build(gi):
        chains = []
        for u in range(unroll):
            c = gi * unroll + u
            q = q_s[c]
            k = k_s[c]
            v = v_s[c]
            gb = gate_ref[0, 0, pl.ds(pl.multiple_of(c * ck, ck), ck), :]
            kb = k.astype(BF16)
            qk_raw = _dot_nt(q.astype(BF16), kb)
            k_t = k.T
            for d in range(2):
                incl = (ii >= jj) if d == 0 else (ii <= jj)
                strict = (ii > jj) if d == 0 else (ii < jj)
                incl_t = (ii <= jj) if d == 0 else (ii >= jj)
                g_col = gb[:, d:d + 1]
                beta = gb[:, 2 + d:3 + d]
                g_row = jnp.sum(jnp.where(eye, g_col, 0.0), axis=0, keepdims=True)
                cum_col = jnp.sum(jnp.where(incl, g_row, 0.0), axis=1, keepdims=True)
                cum_row = jnp.sum(jnp.where(incl_t, g_col, 0.0), axis=0, keepdims=True)
                total = jnp.sum(g_col, axis=0, keepdims=True)
                e = jnp.exp(cum_col - cum_row)
                kk = _dot_nt((beta * k).astype(BF16), kb)
                a = (jnp.where(strict, e, 0.0) * kk).astype(BF16)
                qk_s[d, c] = (jnp.where(incl, e, 0.0) * qk_raw).astype(BF16)
                e_cum = jnp.exp(cum_col)
                rhs = jnp.concatenate([beta * v, (beta * e_cum) * k], axis=1).astype(BF16)
                qd_s[d, c] = (q * e_cum).astype(BF16)
                dec_s[d, c] = jnp.broadcast_to(jnp.exp(total), (1, dk))
                kd_t = (k_t * jnp.exp(total - cum_row)).astype(BF16)
                chains.append([d, c, a, rhs, kd_t, lvl_ref[0] - a * lvl_ref[1]])
        for lvl in range(2, n_lvl + 1):
            xa = [_dot(ch[5], ch[2]).astype(BF16) for ch in chains]
            ys = [_dot(m, ch[5]) for m, ch in zip(xa, chains)]
            for ch, y in zip(chains, ys):
                ch[5] = ch[5] - y.astype(BF16) * lvl_ref[lvl]
        for d, c, _, rhs, kd_t, x in chains:
            sol = _dot(x, rhs)
            u0_s[d, c] = sol[:, :dk]
            w_s[d, c] = sol[:, dk:].astype(BF16)
            nm = _dot(kd_t, sol.astype(BF16))
            n_s[d, c] = nm[:, :dk]
            m_s[d, c] = nm[:, dk:].astype(BF16)

    def group_body(gi, carry):
        build(gi)
        for u in range(unroll):
            prep((gi + 1) * unroll + u)
        return carry

    n_groups = nc // unroll
    for u in range(unroll):
        prep(u)
    lax.fori_loop(0, n_groups - 1, group_body, 0)
    build(n_groups - 1)

    def scan_body(i, carry):
        c_b = jnp.where(i < nl, nl - 1 - i, nc - 1 + nl - i)
        new = []
        for d, c, s in ((0, i, carry[0]), (1, c_b, carry[1])):
            sb = s.astype(BF16)
            st_s[d, c] = sb
            new.append(dec_s[d, c] * s + n_s[d, c] - _dot(m_s[d, c], sb))
        return tuple(new)

    zero = jnp.zeros((dk, dk), F32)
    lax.fori_loop(0, nc, scan_body, (zero, zero))

    def out_body(gi, carry):
        for u in range(unroll):
            c = gi * unroll + u
            rows = pl.ds(pl.multiple_of(c * ck, ck), ck)
            o = jnp.zeros((ck, dk), F32)
            for d in range(2):
                sb = st_s[d, c]
                us = u0_s[d, c] - _dot(w_s[d, c], sb)
                o = o + _dot(qd_s[d, c], sb) + _dot(qk_s[d, c], us.astype(BF16))
            o_ref[0, rows, :] = (_rms(o, onorm_ref[...]) * _silu(z_ref[0, rows, :])).astype(BF16)
        return carry

    lax.fori_loop(0, nc // unroll, out_body, 0)


def gdn_call(qkvz, gates, conv_w, out_norm, *, n_ctx, ck):
    b, t, _ = qkvz.shape
    h, dk = GDN_HEADS, GDN_HEAD_DIM
    nc = t // ck
    unroll = _pick(nc, (3, 2, 1))
    lvl = _level_masks(ck)
    col = lambda off: pl.BlockSpec((1, t, dk), lambda b, hd: (b, 0, off + hd))
    cw = lambda off: pl.BlockSpec((CONV_K, dk), lambda b, hd: (0, off + hd))
    both = lambda r, w, dt: pltpu.VMEM((2, nc, r, w), dt)
    kern = functools.partial(_gdn_kernel, n_ctx=n_ctx, ck=ck, unroll=unroll)
    return pl.pallas_call(
        kern,
        grid=(b, h),
        in_specs=[col(0), col(h), col(2 * h), col(3 * h), cw(0), cw(h), cw(2 * h),
                  pl.BlockSpec((1, 1, t, 4), lambda b, hd: (b, hd, 0, 0)),
                  _const_spec(lvl.shape),
                  pl.BlockSpec((1, dk), lambda b, hd: (0, 0))],
        out_specs=pl.BlockSpec((1, t, dk), lambda b, hd: (b, 0, hd)),
        out_shape=jax.ShapeDtypeStruct((b, t, h * dk), BF16),
        scratch_shapes=[pltpu.VMEM((nc, ck, dk), F32)] * 3
        + [both(ck, dk, F32), both(ck, dk, BF16), both(ck, dk, BF16),
                        both(ck, ck, BF16), both(dk, dk, F32), both(dk, dk, BF16),
                        both(1, dk, F32), both(dk, dk, BF16)],
        compiler_params=_params(("arbitrary", "arbitrary")),
        name="gdn",
    )(qkvz, qkvz, qkvz, qkvz, conv_w, conv_w, conv_w, gates, lvl, out_norm)


def _attn_kernel(q_ref, k_ref, vt_ref, o_ref, *, n_ctx):
    tq = q_ref.shape[1]
    t = k_ref.shape[1]

    def attend(n_keys):
        keys = pl.ds(0, n_keys)
        outs = []
        for g0 in range(0, MLA_HEADS, HEAD_GROUP):
            heads = range(g0, g0 + HEAD_GROUP)
            s_ts = [_dot_nt(k_ref[0, keys, pl.ds(hd * LANES, LANES)],
                            q_ref[0, :, pl.ds(hd * LANES, LANES)]) for hd in heads]
            p_ts = [jnp.exp2(s_t - jnp.max(s_t, axis=0, keepdims=True)) for s_t in s_ts]
            denoms = [jnp.sum(p_t, axis=0, keepdims=True) for p_t in p_ts]
            for hd, p_t, denom in zip(heads, p_ts, denoms):
                o_t = _dot(vt_ref[0, pl.ds(hd * MLA_V, MLA_V), keys], p_t.astype(BF16))
                outs.append(o_t / denom)
        o_ref[0] = jnp.concatenate(outs, axis=0).T.astype(BF16)

    is_ctx = pl.program_id(1) * tq < n_ctx

    @pl.when(is_ctx)
    def _():
        attend(n_ctx)

    @pl.when(jnp.logical_not(is_ctx))
    def _():
        attend(t)


def attn_call(q, k, v_t, *, n_ctx, tq):
    b, t, hw = q.shape
    vw = v_t.shape[1]
    kern = functools.partial(_attn_kernel, n_ctx=n_ctx)
    return pl.pallas_call(
        kern,
        grid=(b, t // tq),
        in_specs=[pl.BlockSpec((1, tq, hw), lambda b, i: (b, i, 0)),
                  pl.BlockSpec((1, t, hw), lambda b, i: (b, 0, 0)),
                  pl.BlockSpec((1, vw, t), lambda b, i: (b, 0, 0))],
        out_specs=pl.BlockSpec((1, tq, vw), lambda b, i: (b, i, 0)),
        out_shape=jax.ShapeDtypeStruct((b, t, vw), BF16),
        compiler_params=_params(("arbitrary", "arbitrary")),
        name="attn",
    )(q, k, v_t)


def _head_pad(w, width):
    kdim = w.shape[0]
    w = w.reshape(kdim, MLA_HEADS, width)
    return jnp.pad(w, ((0, 0), (0, 0), (0, LANES - width))).reshape(kdim, MLA_HEADS * LANES)


def _layout_w_in(w):
    d = w.shape[0]
    off_a = 4 * GDN_WIDTH
    off_b = off_a + 2 * GDN_HEADS
    off_cq = off_b + 2 * GDN_HEADS
    off_ckv = off_cq + Q_RANK
    off_kr = off_ckv + KV_RANK
    small = jnp.concatenate(
        [jnp.zeros((d, ROPE_LANE0), w.dtype), w[:, off_kr:off_kr + MLA_ROPE], w[:, off_a:off_cq],
         jnp.zeros((d, LANES - GATE_LANE0 - 4 * GDN_HEADS), w.dtype)], axis=1)
    return jnp.concatenate([w[:, :off_a], w[:, off_cq:off_kr], small], axis=1).astype(BF16)


def _rope_tables(n_ctx, seq):
    rows = seq // GRID_W
    row = jnp.repeat(jnp.arange(rows), GRID_W).astype(F32)
    col = jnp.tile(jnp.arange(GRID_W), rows).astype(F32)
    axis_dim = MLA_ROPE // 2
    inv_freq = jnp.power(ROPE_BASE, -jnp.arange(0, axis_dim, 2, dtype=F32) / axis_dim)
    ang = jnp.concatenate([row[:, None] * inv_freq, col[:, None] * inv_freq], axis=-1)
    cos, sin = jnp.cos(ang), jnp.sin(ang)
    ones = lambda n: jnp.ones((seq, n), F32)
    zeros = lambda n: jnp.zeros((seq, n), F32)
    tail = LANES - GATE_LANE0
    c = jnp.concatenate([ones(ROPE_LANE0), cos, cos, ones(tail)], axis=1)
    s1 = jnp.concatenate([zeros(ROPE_LANE0 + ROPE_HALF), sin, zeros(tail)], axis=1)
    s2 = jnp.concatenate([zeros(ROPE_LANE0), -sin, zeros(ROPE_HALF + tail)], axis=1)
    ctx = lambda fill: jnp.full((n_ctx, LANES), fill, F32)
    return (jnp.concatenate([ctx(1.0), c]), jnp.concatenate([ctx(0.0), s1]),
            jnp.concatenate([ctx(0.0), s2]))


def kernel(x, c, ctx, c_ctx, w_ada, b_ada, norm_pre, norm_post, ffn_w_gate, ffn_w_up, ffn_w_down,
           w_in, gdn_conv, gdn_a_log, gdn_dt_bias, gdn_out_norm, mla_q_norm, mla_kv_norm, mla_w_uq,
           mla_w_ukv, w_out):
    bsz, seq, d = x.shape
    n_ctx = ctx.shape[1]
    t = n_ctx + seq
    depth = w_ada.shape[0]
    tm = _pick(t, (768, 512, 384, 256, 128))
    tq = _pick(n_ctx, (256, 128))
    ck = tq
    assert seq % ck == 0 and seq % GRID_W == 0

    cond_rows = -(-(bsz + 1) // SUBLANES) * SUBLANES
    cond = jnp.concatenate([c, c_ctx[None], jnp.zeros((cond_rows - bsz - 1, d), F32)], axis=0)
    mod = ada_call(cond, w_ada, b_ada).reshape(depth, cond_rows, N_MOD, d)
    rope_c, rope_s1, rope_s2 = _rope_tables(n_ctx, seq)

    xs = jnp.concatenate([ctx, x], axis=1)
    for l in range(depth):
        row = lambda v: v.reshape(1, -1)
        ffn_w = lambda j: (ffn_w_gate[l, j].astype(BF16), ffn_w_up[l, j].astype(BF16),
                           ffn_w_down[l, j].astype(BF16))
        xs = ffn_call(xs, mod[l], row(norm_pre[l, 0]), row(norm_post[l, 0]), *ffn_w(0),
                      slot=0, n_ctx=n_ctx, tm=tm)

        uq = mla_w_uq[l].reshape(Q_RANK, MLA_HEADS, MLA_NOPE + MLA_ROPE)
        ukv = mla_w_ukv[l].reshape(KV_RANK, MLA_HEADS, MLA_NOPE + MLA_V)
        wuq = _head_pad(uq.reshape(Q_RANK, -1), MLA_NOPE + MLA_ROPE).astype(BF16)
        wuk = _head_pad(ukv[:, :, :MLA_NOPE].reshape(KV_RANK, -1), MLA_NOPE).astype(BF16)
        wuv = ukv[:, :, MLA_NOPE:].reshape(KV_RANK, -1).astype(BF16)
        gpar = jnp.pad(jnp.stack([gdn_a_log[l].reshape(-1), gdn_dt_bias[l].reshape(-1)]),
                       ((0, 0), (GATE_LANE0, LANES - GATE_LANE0 - 2 * GDN_HEADS)))
        qkvz, small, q, k, v = mixin_call(
            xs, mod[l], row(norm_pre[l, 1]), _layout_w_in(w_in[l]), row(mla_q_norm[l]),
            row(mla_kv_norm[l]), wuq, wuk, wuv, gpar, rope_c, rope_s1, rope_s2, n_ctx=n_ctx, tm=tm)

        ab = small[:, :, GATE_LANE0:GATE_LANE0 + 4 * GDN_HEADS]
        gates = ab.reshape(bsz, t, 4, GDN_HEADS).transpose(0, 3, 1, 2)
        o_gdn = gdn_call(qkvz, gates, gdn_conv[l], row(gdn_out_norm[l]), n_ctx=n_ctx, ck=ck)
        o_mla = attn_call(q, k, v.transpose(0, 2, 1), n_ctx=n_ctx, tq=tq)
        xs = mixffn_call(xs, o_gdn, o_mla, mod[l], row(norm_post[l, 1]),
                         w_out[l, :GDN_WIDTH].astype(BF16), w_out[l, GDN_WIDTH:].astype(BF16),
                         row(norm_pre[l, 2]), row(norm_post[l, 2]), *ffn_w(1), n_ctx=n_ctx, tm=tm)
    return xs[:, n_ctx:]
```

```python
import functools
import math

import jax
import jax.numpy as jnp
from jax import lax
from jax.experimental import pallas as pl
from jax.experimental.pallas import tpu as pltpu

F32 = jnp.float32
BF16 = jnp.bfloat16

EPS = 1e-6
N_MOD = 9
GDN_HEADS = 4
GDN_HEAD_DIM = 128
GDN_WIDTH = GDN_HEADS * GDN_HEAD_DIM
CONV_K = 5
CONV_PAD = CONV_K // 2
MLA_HEADS = 8
MLA_NOPE = 64
MLA_ROPE = 32
MLA_V = 64
Q_RANK = 384
KV_RANK = 256
ROPE_BASE = 10000.0
GRID_W = 64
MLA_SCALE = (MLA_NOPE + MLA_ROPE) ** -0.5
LOG2_E = math.log2(math.e)

LANES = 128
SUBLANES = 8
ROPE_HALF = MLA_ROPE // 2
ROPE_LANE0 = MLA_NOPE
GATE_LANE0 = ROPE_LANE0 + MLA_ROPE
HEAD_GROUP = 4
BF16_ROWS = 16
VMEM_LIMIT = 56 * 1024 * 1024


def _pick(n, candidates):
    for cand in candidates:
        if n % cand == 0:
            return cand
    raise ValueError(f"no tile for {n}")


def _sigmoid(x):
    return 1.0 / (1.0 + jnp.exp(-x))


def _silu(x):
    return x * _sigmoid(x)


def _rms(x, gain):
    return x * lax.rsqrt(jnp.mean(x * x, axis=-1, keepdims=True) + EPS) * gain


def _dot(a, b):
    return jnp.dot(a, b, preferred_element_type=F32)


def _dot_nt(a, b):
    return lax.dot_general(a, b, (((1,), (1,)), ((), ())), preferred_element_type=F32)


def _const_spec(shape):
    zeros = (0,) * len(shape)
    return pl.BlockSpec(shape, lambda *_: zeros, pipeline_mode=pl.Buffered(1))


def _params(sem):
    return pltpu.CompilerParams(dimension_semantics=sem, vmem_limit_bytes=VMEM_LIMIT)


def _ada_kernel(s_ref, w_ref, b_ref, o_ref):
    s = _silu(s_ref[...])
    o_ref[0] = jnp.dot(s, w_ref[0], precision=lax.Precision.HIGHEST,
                       preferred_element_type=F32) + b_ref[0]


def ada_call(cond, w_ada, b_ada):
    depth, d, n = w_ada.shape
    rows = cond.shape[0]
    tn = _pick(n, (2304, 1024, 512, 256, 128))
    return pl.pallas_call(
        _ada_kernel,
        grid=(depth, n // tn),
        in_specs=[pl.BlockSpec((rows, d), lambda l, j: (0, 0)),
                  pl.BlockSpec((1, d, tn), lambda l, j: (l, 0, j)),
                  pl.BlockSpec((1, 1, tn), lambda l, j: (l, 0, j))],
        out_specs=pl.BlockSpec((1, rows, tn), lambda l, j: (l, 0, j)),
        out_shape=jax.ShapeDtypeStruct((depth, rows, n), F32),
        compiler_params=_params(("arbitrary", "arbitrary")),
        name="ada",
    )(cond, w_ada, b_ada.reshape(depth, 1, n))


def _mod_rows(ml_ref, mc_ref, slot, is_ctx):
    out = []
    for j in range(3):
        lat = ml_ref[0, pl.ds(3 * slot + j, 1), :]
        ctx = mc_ref[0, pl.ds(3 * slot + j, 1), :]
        out.append(jnp.where(is_ctx, ctx, lat))
    return out


def _is_ctx_rows(tm, n_ctx, tile0=0):
    rows = (pl.program_id(1) + tile0) * tm + lax.broadcasted_iota(jnp.int32, (tm, 1), 0)
    return rows < n_ctx


def _mod_specs(n_ctx_row, d):
    return [pl.BlockSpec((1, N_MOD, d), lambda b, i: (b, 0, 0)),
            pl.BlockSpec((1, N_MOD, d), lambda b, i: (n_ctx_row, 0, 0))]


def _ffn_tile(x, is_ctx, ml_ref, mc_ref, gpre_ref, gpost_ref, wg_ref, wu_ref, wd_ref, slot, fc):
    shift, scale, gate = _mod_rows(ml_ref, mc_ref, slot, is_ctx)
    h = (_rms(x, gpre_ref[...]) * (1.0 + scale) + shift).astype(BF16)
    y = jnp.zeros(x.shape, F32)
    for j in range(wg_ref.shape[1] // fc):
        cols = pl.ds(j * fc, fc)
        act = _silu(_dot(h, wg_ref[:, cols])) * _dot(h, wu_ref[:, cols])
        y = y + _dot(act.astype(BF16), wd_ref[cols, :])
    return x + 0.5 * gate * _rms(y, gpost_ref[...])


def _ffn_kernel(x_ref, ml_ref, mc_ref, gpre_ref, gpost_ref, wg_ref, wu_ref, wd_ref, o_ref,
                *, slot, n_ctx, fc):
    is_ctx = _is_ctx_rows(x_ref.shape[1], n_ctx)
    o_ref[0] = _ffn_tile(x_ref[0], is_ctx, ml_ref, mc_ref, gpre_ref, gpost_ref, wg_ref, wu_ref,
                         wd_ref, slot, fc)


def _ffn_weight_specs(d, dff):
    return [_const_spec((1, d)), _const_spec((1, d)),
            _const_spec((d, dff)), _const_spec((d, dff)), _const_spec((dff, d))]


def ffn_call(x, mod, gpre, gpost, wg, wu, wd, *, slot, n_ctx, tm):
    b, t, d = x.shape
    dff = wg.shape[1]
    kern = functools.partial(_ffn_kernel, slot=slot, n_ctx=n_ctx, fc=_pick(dff, (256, 128)))
    tok = pl.BlockSpec((1, tm, d), lambda b, i: (b, i, 0))
    return pl.pallas_call(
        kern,
        grid=(b, t // tm),
        in_specs=[tok] + _mod_specs(b, d) + _ffn_weight_specs(d, dff),
        out_specs=tok,
        out_shape=jax.ShapeDtypeStruct(x.shape, F32),
        compiler_params=_params(("arbitrary", "arbitrary")),
        name="ffn",
    )(x, mod, mod, gpre, gpost, wg, wu, wd)


def _mixffn_kernel(x_ref, og_ref, om_ref, ml_ref, mc_ref, gmix_ref, wog_ref, wom_ref,
                   gpre_ref, gpost_ref, wg_ref, wu_ref, wd_ref, o_ref, *, n_ctx, fc, tile0):
    is_ctx = _is_ctx_rows(x_ref.shape[1], n_ctx, tile0)
    _, _, gate = _mod_rows(ml_ref, mc_ref, 1, is_ctx)
    y = _dot(og_ref[0], wog_ref[...]) + _dot(om_ref[0], wom_ref[...])
    x = x_ref[0] + gate * _rms(y, gmix_ref[...])
    o_ref[0] = _ffn_tile(x, is_ctx, ml_ref, mc_ref, gpre_ref, gpost_ref, wg_ref, wu_ref, wd_ref, 2,
                         fc)


def mixffn_call(x, o_gdn, o_mla, mod, gmix, w_gdn, w_mla, gpre, gpost, wg, wu, wd, *, n_ctx, tm,
                latent_only):
    b, t, d = x.shape
    dff = wg.shape[1]
    tile0 = n_ctx // tm if latent_only else 0
    tok = lambda w: pl.BlockSpec((1, tm, w), lambda b, i: (b, i + tile0, 0))
    kern = functools.partial(_mixffn_kernel, n_ctx=n_ctx, fc=_pick(dff, (256, 128)), tile0=tile0)
    return pl.pallas_call(
        kern,
        grid=(b, t // tm - tile0),
        in_specs=[tok(d), tok(o_gdn.shape[2]), tok(o_mla.shape[2])] + _mod_specs(b, d)
        + [_const_spec((1, d)), _const_spec(w_gdn.shape), _const_spec(w_mla.shape)]
        + _ffn_weight_specs(d, dff),
        out_specs=pl.BlockSpec((1, tm, d), lambda b, i: (b, i, 0)),
        out_shape=jax.ShapeDtypeStruct((b, t - tile0 * tm, d), F32),
        compiler_params=_params(("arbitrary", "arbitrary")),
        name="mixffn",
    )(x, o_gdn, o_mla, mod, mod, gmix, w_gdn, w_mla, gpre, gpost, wg, wu, wd)


def _rope(x, c_ref, s1_ref, s2_ref):
    return (x * c_ref[...] + pltpu.roll(x, ROPE_HALF, 1) * s1_ref[...]
            + pltpu.roll(x, LANES - ROPE_HALF, 1) * s2_ref[...])


def _mixin_kernel(x_ref, ml_ref, mc_ref, gpre_ref, win_ref, qn_ref, kvn_ref, wuq_ref, wuk_ref,
                  wuvt_ref, gpar_ref, c_ref, s1_ref, s2_ref,
                  qkvz_ref, small_ref, q_ref, k_ref, vt_ref, *, n_ctx):
    tm = x_ref.shape[1]
    shift, scale, _ = _mod_rows(ml_ref, mc_ref, 1, _is_ctx_rows(tm, n_ctx))
    h = (_rms(x_ref[0], gpre_ref[...]) * (1.0 + scale) + shift).astype(BF16)
    n_gdn = qkvz_ref.shape[2]
    qkvz_ref[0] = _dot(h, win_ref[:, pl.ds(0, n_gdn)]).astype(BF16)
    rest = _dot(h, win_ref[:, pl.ds(n_gdn, Q_RANK + KV_RANK + LANES)])
    c_q = rest[:, :Q_RANK]
    c_kv = rest[:, Q_RANK:Q_RANK + KV_RANK]
    small = _rope(rest[:, Q_RANK + KV_RANK:], c_ref, s1_ref, s2_ref)
    lane = lax.broadcasted_iota(jnp.int32, (1, LANES), 1)
    pre = small + gpar_ref[pl.ds(1, 1), :]
    softplus = jnp.maximum(pre, 0.0) + jnp.log(1.0 + jnp.exp(-jnp.abs(pre)))
    log_decay = -jnp.exp(gpar_ref[pl.ds(0, 1), :]) * softplus
    n_dec = 2 * GDN_HEADS
    is_dec = (lane >= GATE_LANE0) & (lane < GATE_LANE0 + n_dec)
    is_beta = (lane >= GATE_LANE0 + n_dec) & (lane < GATE_LANE0 + 2 * n_dec)
    small_ref[0] = jnp.where(is_dec, log_decay, jnp.where(is_beta, _sigmoid(small), small))
    k_rope = jnp.where((lane >= ROPE_LANE0) & (lane < GATE_LANE0), small, 0.0)

    q = _dot(_rms(c_q, qn_ref[...]).astype(BF16), wuq_ref[...]) * (MLA_SCALE * LOG2_E)
    ckv_n = _rms(c_kv, kvn_ref[...]).astype(BF16)
    k = _dot(ckv_n, wuk_ref[...])
    for hd in range(MLA_HEADS):
        cols = pl.ds(hd * LANES, LANES)
        blk = slice(hd * LANES, (hd + 1) * LANES)
        q_ref[0, :, cols] = _rope(q[:, blk], c_ref, s1_ref, s2_ref).astype(BF16)
        k_ref[0, :, cols] = (k[:, blk] + k_rope).astype(BF16)
    vt_ref[0] = _dot_nt(wuvt_ref[...], ckv_n).astype(BF16)


def mixin_call(x, mod, gpre, win, qn, kvn, wuq, wuk, wuv_t, gpar, rope_c, rope_s1, rope_s2, *, n_ctx,
               tm):
    b, t, d = x.shape
    n_gdn = 4 * GDN_WIDTH
    hw = MLA_HEADS * LANES
    vrows = wuv_t.shape[0]
    tok = lambda w: pl.BlockSpec((1, tm, w), lambda b, i: (b, i, 0))
    tab = pl.BlockSpec((tm, LANES), lambda b, i: (i, 0))
    kern = functools.partial(_mixin_kernel, n_ctx=n_ctx)
    return pl.pallas_call(
        kern,
        grid=(b, t // tm),
        in_specs=[tok(d)] + _mod_specs(b, d)
        + [_const_spec((1, d)), _const_spec(win.shape), _const_spec((1, Q_RANK)),
           _const_spec((1, KV_RANK)), _const_spec(wuq.shape), _const_spec(wuk.shape),
           _const_spec(wuv_t.shape), _const_spec(gpar.shape), tab, tab, tab],
        out_specs=[tok(n_gdn), tok(LANES), tok(hw), tok(hw),
                   pl.BlockSpec((1, vrows, tm), lambda b, i: (b, 0, i))],
        out_shape=[jax.ShapeDtypeStruct((b, t, n_gdn), BF16),
                   jax.ShapeDtypeStruct((b, t, LANES), F32),
                   jax.ShapeDtypeStruct((b, t, hw), BF16),
                   jax.ShapeDtypeStruct((b, t, hw), BF16),
                   jax.ShapeDtypeStruct((b, vrows, t), BF16)],
        compiler_params=_params(("arbitrary", "arbitrary")),
        name="mixin",
    )(x, mod, mod, gpre, win, qn, kvn, wuq, wuk, wuv_t, gpar, rope_c, rope_s1, rope_s2)


def _level_masks(ck):
    ii = lax.broadcasted_iota(jnp.int32, (ck, ck), 0)
    jj = lax.broadcasted_iota(jnp.int32, (ck, ck), 1)
    masks = [ii == jj]
    for lvl in range(1, ck.bit_length()):
        masks.append(((ii >> lvl) == (jj >> lvl)) & ((ii >> (lvl - 1)) != (jj >> (lvl - 1))))
    return jnp.stack(masks).astype(BF16)


def _gdn_kernel(q_ref, k_ref, v_ref, z_ref, cq_ref, ck_ref, cv_ref, gate_ref, lvl_ref, onorm_ref,
                o_ref, q_s, k_s, v_s, u0_s, w_s, qd_s, qk_s, n_s, m_s, dec_s, st_s,
                *, n_ctx, ck, unroll):
    t = q_ref.shape[1]
    dk = q_ref.shape[2]
    nc = t // ck
    nl = n_ctx // ck
    n_lvl = lvl_ref.shape[0] - 1
    halo = BF16_ROWS

    ii = lax.broadcasted_iota(jnp.int32, (ck, ck), 0)
    jj = lax.broadcasted_iota(jnp.int32, (ck, ck), 1)
    eye = ii == jj

    def conv(u_ref, w_ref, c):
        r0 = pl.multiple_of(c * ck, ck)
        lo = pl.multiple_of(jnp.maximum(r0 - halo, 0), halo)
        hi = pl.multiple_of(jnp.minimum(r0 + ck, t - halo), halo)
        keep_lo = jnp.where((c == 0) | (c == nl), 0.0, 1.0)
        keep_hi = jnp.where((c == nl - 1) | (c == nc - 1), 0.0, 1.0)
        full = jnp.concatenate([u_ref[0, pl.ds(lo, halo), :].astype(F32) * keep_lo,
                                u_ref[0, pl.ds(r0, ck), :].astype(F32),
                                u_ref[0, pl.ds(hi, halo), :].astype(F32) * keep_hi], axis=0)
        acc = full[halo:halo + ck] * w_ref[pl.ds(CONV_PAD, 1), :]
        for s in range(-CONV_PAD, CONV_PAD + 1):
            if s != 0:
                acc = acc + full[halo + s:halo + s + ck] * w_ref[pl.ds(CONV_PAD + s, 1), :]
        return _silu(acc)

    def l2n(u):
        return u * lax.rsqrt(jnp.sum(u * u, axis=-1, keepdims=True) + EPS)

    def prep(c):
        q_s[c] = l2n(conv(q_ref, cq_ref, c)) * (dk ** -0.5)
        k_s[c] = l2n(conv(k_ref, ck_ref, c))
        v_s[c] = conv(v_ref, cv_ref, c)

    def build(gi):
        chains = []
        for u in range(unroll):
            c = gi * unroll + u
            q = q_s[c]
            k = k_s[c]
            v = v_s[c]
            gb = gate_ref[0, 0, pl.ds(pl.multiple_of(c * ck, ck), ck), :]
            kb = k.astype(BF16)
            qk_raw = _dot_nt(q.astype(BF16), kb)
            k_t = k.T
            for d in range(2):
                incl = (ii >= jj) if d == 0 else (ii <= jj)
                strict = (ii > jj) if d == 0 else (ii < jj)
                incl_t = (ii <= jj) if d == 0 else (ii >= jj)
                g_col = gb[:, d:d + 1]
                beta = gb[:, 2 + d:3 + d]
                g_row = jnp.sum(jnp.where(eye, g_col, 0.0), axis=0, keepdims=True)
                cum_col = jnp.sum(jnp.where(incl, g_row, 0.0), axis=1, keepdims=True)
                cum_row = jnp.sum(jnp.where(incl_t, g_col, 0.0), axis=0, keepdims=True)
                total = jnp.sum(g_col, axis=0, keepdims=True)
                e = jnp.exp(cum_col - cum_row)
                kk = _dot_nt((beta * k).astype(BF16), kb)
                a = (jnp.where(strict, e, 0.0) * kk).astype(BF16)
                qk_s[d, c] = (jnp.where(incl, e, 0.0) * qk_raw).astype(BF16)
                e_cum = jnp.exp(cum_col)
                rhs = jnp.concatenate([beta * v, (beta * e_cum) * k], axis=1).astype(BF16)
                qd_s[d, c] = (q * e_cum).astype(BF16)
                dec_s[d, c] = jnp.broadcast_to(jnp.exp(total), (1, dk))
                kd_t = (k_t * jnp.exp(total - cum_row)).astype(BF16)
                chains.append([d, c, a, rhs, kd_t, lvl_ref[0] - a * lvl_ref[1]])
        for lvl in range(2, n_lvl + 1):
            xa = [_dot(ch[5], ch[2]).astype(BF16) for ch in chains]
            ys = [_dot(m, ch[5]) for m, ch in zip(xa, chains)]
            for ch, y in zip(chains, ys):
                ch[5] = ch[5] - y.astype(BF16) * lvl_ref[lvl]
        for d, c, _, rhs, kd_t, x in chains:
            sol = _dot(x, rhs)
            u0_s[d, c] = sol[:, :dk]
            w_s[d, c] = sol[:, dk:].astype(BF16)
            nm = _dot(kd_t, sol.astype(BF16))
            n_s[d, c] = nm[:, :dk]
            m_s[d, c] = nm[:, dk:].astype(BF16)

    def group_body(gi, carry):
        build(gi)
        for u in range(unroll):
            prep((gi + 1) * unroll + u)
        return carry

    n_groups = nc // unroll
    for u in range(unroll):
        prep(u)
    lax.fori_loop(0, n_groups - 1, group_body, 0)
    build(n_groups - 1)

    def scan_body(i, carry):
        c_b = jnp.where(i < nl, nl - 1 - i, nc - 1 + nl - i)
        new = []
        for d, c, s in ((0, i, carry[0]), (1, c_b, carry[1])):
            sb = s.astype(BF16)
            st_s[d, c] = sb
            new.append(dec_s[d, c] * s + n_s[d, c] - _dot(m_s[d, c], sb))
        return tuple(new)

    zero = jnp.zeros((dk, dk), F32)
    lax.fori_loop(0, nc, scan_body, (zero, zero))

    def out_body(gi, carry):
        for u in range(unroll):
            c = gi * unroll + u
            rows = pl.ds(pl.multiple_of(c * ck, ck), ck)
            o = jnp.zeros((ck, dk), F32)
            for d in range(2):
                sb = st_s[d, c]
                us = u0_s[d, c] - _dot(w_s[d, c], sb)
                o = o + _dot(qd_s[d, c], sb) + _dot(qk_s[d, c], us.astype(BF16))
            z = z_ref[0, rows, :].astype(F32)
            o_ref[0, rows, :] = (_rms(o, onorm_ref[...]) * _silu(z)).astype(BF16)
        return carry

    lax.fori_loop(0, nc // unroll, out_body, 0)


def gdn_call(qkvz, gates, conv_w, out_norm, *, n_ctx, ck):
    b, t, _ = qkvz.shape
    h, dk = GDN_HEADS, GDN_HEAD_DIM
    nc = t // ck
    unroll = _pick(nc, (3, 2, 1))
    lvl = _level_masks(ck)
    col = lambda off: pl.BlockSpec((1, t, dk), lambda b, hd: (b, 0, off + hd))
    cw = lambda off: pl.BlockSpec((CONV_K, dk), lambda b, hd: (0, off + hd))
    both = lambda r, w, dt: pltpu.VMEM((2, nc, r, w), dt)
    kern = functools.partial(_gdn_kernel, n_ctx=n_ctx, ck=ck, unroll=unroll)
    return pl.pallas_call(
        kern,
        grid=(b, h),
        in_specs=[col(0), col(h), col(2 * h), col(3 * h), cw(0), cw(h), cw(2 * h),
                  pl.BlockSpec((1, 1, t, 4), lambda b, hd: (b, hd, 0, 0)),
                  _const_spec(lvl.shape),
                  pl.BlockSpec((1, dk), lambda b, hd: (0, 0))],
        out_specs=pl.BlockSpec((1, t, dk), lambda b, hd: (b, 0, hd)),
        out_shape=jax.ShapeDtypeStruct((b, t, h * dk), BF16),
        scratch_shapes=[pltpu.VMEM((nc, ck, dk), F32)] * 3
        + [both(ck, dk, F32), both(ck, dk, BF16), both(ck, dk, BF16),
                        both(ck, ck, BF16), both(dk, dk, F32), both(dk, dk, BF16),
                        both(1, dk, F32), both(dk, dk, BF16)],
        compiler_params=_params(("arbitrary", "arbitrary")),
        name="gdn",
    )(qkvz, qkvz, qkvz, qkvz, conv_w, conv_w, conv_w, gates, lvl, out_norm)


def _attn_kernel(q_ref, k_ref, vt_ref, o_ref, *, n_ctx):
    tq = q_ref.shape[1]
    t = k_ref.shape[1]

    def attend(n_keys):
        keys = pl.ds(0, n_keys)
        outs = []
        for g0 in range(0, MLA_HEADS, HEAD_GROUP):
            heads = range(g0, g0 + HEAD_GROUP)
            s_ts = [_dot_nt(k_ref[0, keys, pl.ds(hd * LANES, LANES)],
                            q_ref[0, :, pl.ds(hd * LANES, LANES)]) for hd in heads]
            p_ts = [jnp.exp2(s_t - jnp.max(s_t, axis=0, keepdims=True)) for s_t in s_ts]
            denoms = [jnp.sum(p_t, axis=0, keepdims=True) for p_t in p_ts]
            for hd, p_t, denom in zip(heads, p_ts, denoms):
                o_t = _dot(vt_ref[0, pl.ds(hd * MLA_V, MLA_V), keys], p_t.astype(BF16))
                outs.append(o_t / denom)
        o_ref[0] = jnp.concatenate(outs, axis=0).T.astype(BF16)

    is_ctx = pl.program_id(1) * tq < n_ctx

    @pl.when(is_ctx)
    def _():
        attend(n_ctx)

    @pl.when(jnp.logical_not(is_ctx))
    def _():
        attend(t)


def attn_call(q, k, v_t, *, n_ctx, tq):
    b, t, hw = q.shape
    vw = MLA_HEADS * MLA_V
    kern = functools.partial(_attn_kernel, n_ctx=n_ctx)
    return pl.pallas_call(
        kern,
        grid=(b, t // tq),
        in_specs=[pl.BlockSpec((1, tq, hw), lambda b, i: (b, i, 0)),
                  pl.BlockSpec((1, t, hw), lambda b, i: (b, 0, 0)),
                  pl.BlockSpec((1, v_t.shape[1], t), lambda b, i: (b, 0, 0))],
        out_specs=pl.BlockSpec((1, tq, vw), lambda b, i: (b, i, 0)),
        out_shape=jax.ShapeDtypeStruct((b, t, vw), BF16),
        compiler_params=_params(("arbitrary", "arbitrary")),
        name="attn",
    )(q, k, v_t)


def _head_pad(w, width):
    kdim = w.shape[0]
    w = w.reshape(kdim, MLA_HEADS, width)
    return jnp.pad(w, ((0, 0), (0, 0), (0, LANES - width))).reshape(kdim, MLA_HEADS * LANES)


def _layout_w_in(w):
    d = w.shape[0]
    off_a = 4 * GDN_WIDTH
    off_b = off_a + 2 * GDN_HEADS
    off_cq = off_b + 2 * GDN_HEADS
    off_ckv = off_cq + Q_RANK
    off_kr = off_ckv + KV_RANK
    small = jnp.concatenate(
        [jnp.zeros((d, ROPE_LANE0), w.dtype), w[:, off_kr:off_kr + MLA_ROPE], w[:, off_a:off_cq],
         jnp.zeros((d, LANES - GATE_LANE0 - 4 * GDN_HEADS), w.dtype)], axis=1)
    return jnp.concatenate([w[:, :off_a], w[:, off_cq:off_kr], small], axis=1).astype(BF16)


def _rope_tables(n_ctx, seq):
    rows = seq // GRID_W
    row = jnp.repeat(jnp.arange(rows), GRID_W).astype(F32)
    col = jnp.tile(jnp.arange(GRID_W), rows).astype(F32)
    axis_dim = MLA_ROPE // 2
    inv_freq = jnp.power(ROPE_BASE, -jnp.arange(0, axis_dim, 2, dtype=F32) / axis_dim)
    ang = jnp.concatenate([row[:, None] * inv_freq, col[:, None] * inv_freq], axis=-1)
    cos, sin = jnp.cos(ang), jnp.sin(ang)
    ones = lambda n: jnp.ones((seq, n), F32)
    zeros = lambda n: jnp.zeros((seq, n), F32)
    tail = LANES - GATE_LANE0
    c = jnp.concatenate([ones(ROPE_LANE0), cos, cos, ones(tail)], axis=1)
    s1 = jnp.concatenate([zeros(ROPE_LANE0 + ROPE_HALF), sin, zeros(tail)], axis=1)
    s2 = jnp.concatenate([zeros(ROPE_LANE0), -sin, zeros(ROPE_HALF + tail)], axis=1)
    ctx = lambda fill: jnp.full((n_ctx, LANES), fill, F32)
    return (jnp.concatenate([ctx(1.0), c]), jnp.concatenate([ctx(0.0), s1]),
            jnp.concatenate([ctx(0.0), s2]))


def kernel(x, c, ctx, c_ctx, w_ada, b_ada, norm_pre, norm_post, ffn_w_gate, ffn_w_up, ffn_w_down,
           w_in, gdn_conv, gdn_a_log, gdn_dt_bias, gdn_out_norm, mla_q_norm, mla_kv_norm, mla_w_uq,
           mla_w_ukv, w_out):
    bsz, seq, d = x.shape
    n_ctx = ctx.shape[1]
    t = n_ctx + seq
    depth = w_ada.shape[0]
    tm = _pick(t, (768, 512, 384, 256, 128))
    tq = _pick(n_ctx, (256, 128))
    ck = tq
    assert seq % ck == 0 and seq % GRID_W == 0

    cond_rows = -(-(bsz + 1) // SUBLANES) * SUBLANES
    cond = jnp.concatenate([c, c_ctx[None], jnp.zeros((cond_rows - bsz - 1, d), F32)], axis=0)
    mod = ada_call(cond, w_ada, b_ada).reshape(depth, cond_rows, N_MOD, d)
    rope_c, rope_s1, rope_s2 = _rope_tables(n_ctx, seq)

    xs = jnp.concatenate([ctx, x], axis=1)
    for l in range(depth):
        last = l == depth - 1
        row = lambda v: v.reshape(1, -1)
        ffn_w = lambda j: (ffn_w_gate[l, j].astype(BF16), ffn_w_up[l, j].astype(BF16),
                           ffn_w_down[l, j].astype(BF16))
        xs = ffn_call(xs, mod[l], row(norm_pre[l, 0]), row(norm_post[l, 0]), *ffn_w(0),
                      slot=0, n_ctx=n_ctx, tm=tm)

        uq = mla_w_uq[l].reshape(Q_RANK, MLA_HEADS, MLA_NOPE + MLA_ROPE)
        ukv = mla_w_ukv[l].reshape(KV_RANK, MLA_HEADS, MLA_NOPE + MLA_V)
        wuq = _head_pad(uq.reshape(Q_RANK, -1), MLA_NOPE + MLA_ROPE).astype(BF16)
        wuk = _head_pad(ukv[:, :, :MLA_NOPE].reshape(KV_RANK, -1), MLA_NOPE).astype(BF16)
        wuv_t = ukv[:, :, MLA_NOPE:].reshape(KV_RANK, MLA_HEADS * MLA_V).T.astype(BF16)
        gpar = jnp.pad(jnp.stack([gdn_a_log[l].reshape(-1), gdn_dt_bias[l].reshape(-1)]),
                       ((0, 0), (GATE_LANE0, LANES - GATE_LANE0 - 2 * GDN_HEADS)))
        qkvz, small, q, k, v_t = mixin_call(
            xs, mod[l], row(norm_pre[l, 1]), _layout_w_in(w_in[l]), row(mla_q_norm[l]),
            row(mla_kv_norm[l]), wuq, wuk, wuv_t, gpar, rope_c, rope_s1, rope_s2, n_ctx=n_ctx,
            tm=tm)

        ab = small[:, :, GATE_LANE0:GATE_LANE0 + 4 * GDN_HEADS]
        gates = ab.reshape(bsz, t, 4, GDN_HEADS).transpose(0, 3, 1, 2)
        o_gdn = gdn_call(qkvz, gates, gdn_conv[l], row(gdn_out_norm[l]), n_ctx=n_ctx, ck=ck)
        o_mla = attn_call(q, k, v_t, n_ctx=n_ctx, tq=tq)
        xs = mixffn_call(xs, o_gdn, o_mla, mod[l], row(norm_post[l, 1]),
                         w_out[l, :GDN_WIDTH].astype(BF16), w_out[l, GDN_WIDTH:].astype(BF16),
                         row(norm_pre[l, 2]), row(norm_post[l, 2]), *ffn_w(1), n_ctx=n_ctx,
                         tm=tq if last else tm, latent_only=last)
    return xs
```

```python
import functools
import math

import jax
import jax.numpy as jnp
from jax import lax
from jax.experimental import pallas as pl
from jax.experimental.pallas import tpu as pltpu

F32 = jnp.float32
BF16 = jnp.bfloat16

EPS = 1e-6
N_MOD = 9
GDN_HEADS = 4
GDN_HEAD_DIM = 128
GDN_WIDTH = GDN_HEADS * GDN_HEAD_DIM
CONV_K = 5
CONV_PAD = CONV_K // 2
MLA_HEADS = 8
MLA_NOPE = 64
MLA_ROPE = 32
MLA_V = 64
Q_RANK = 384
KV_RANK = 256
ROPE_BASE = 10000.0
GRID_W = 64
MLA_SCALE = (MLA_NOPE + MLA_ROPE) ** -0.5
LOG2_E = math.log2(math.e)

LANES = 128
SUBLANES = 8
ROPE_HALF = MLA_ROPE // 2
ROPE_LANE0 = MLA_NOPE
GATE_LANE0 = ROPE_LANE0 + MLA_ROPE
HEAD_GROUP = 4
BF16_ROWS = 16
VMEM_LIMIT = 56 * 1024 * 1024


def _pick(n, candidates):
    for cand in candidates:
        if n % cand == 0:
            return cand
    raise ValueError(f"no tile for {n}")


def _sigmoid(x):
    return 1.0 / (1.0 + jnp.exp(-x))


def _silu(x):
    return x * _sigmoid(x)


def _rms(x, gain):
    return x * lax.rsqrt(jnp.mean(x * x, axis=-1, keepdims=True) + EPS) * gain


def _dot(a, b):
    return jnp.dot(a, b, preferred_element_type=F32)


def _dot_nt(a, b):
    return lax.dot_general(a, b, (((1,), (1,)), ((), ())), preferred_element_type=F32)


def _const_spec(shape):
    zeros = (0,) * len(shape)
    return pl.BlockSpec(shape, lambda *_: zeros, pipeline_mode=pl.Buffered(1))


def _params(sem):
    return pltpu.CompilerParams(dimension_semantics=sem, vmem_limit_bytes=VMEM_LIMIT)


def _ada_kernel(s_ref, w_ref, b_ref, o_ref):
    s = _silu(s_ref[...])
    o_ref[0] = jnp.dot(s, w_ref[0], precision=lax.Precision.HIGHEST,
                       preferred_element_type=F32) + b_ref[0]


def ada_call(cond, w_ada, b_ada):
    depth, d, n = w_ada.shape
    rows = cond.shape[0]
    tn = _pick(n, (2304, 1024, 512, 256, 128))
    return pl.pallas_call(
        _ada_kernel,
        grid=(depth, n // tn),
        in_specs=[pl.BlockSpec((rows, d), lambda l, j: (0, 0)),
                  pl.BlockSpec((1, d, tn), lambda l, j: (l, 0, j)),
                  pl.BlockSpec((1, 1, tn), lambda l, j: (l, 0, j))],
        out_specs=pl.BlockSpec((1, rows, tn), lambda l, j: (l, 0, j)),
        out_shape=jax.ShapeDtypeStruct((depth, rows, n), F32),
        compiler_params=_params(("arbitrary", "arbitrary")),
        name="ada",
    )(cond, w_ada, b_ada.reshape(depth, 1, n))


def _mod_rows(ml_ref, mc_ref, slot, is_ctx):
    out = []
    for j in range(3):
        lat = ml_ref[0, pl.ds(3 * slot + j, 1), :]
        ctx = mc_ref[0, pl.ds(3 * slot + j, 1), :]
        out.append(jnp.where(is_ctx, ctx, lat))
    return out


def _is_ctx_rows(tm, n_ctx):
    rows = pl.program_id(1) * tm + lax.broadcasted_iota(jnp.int32, (tm, 1), 0)
    return rows < n_ctx


def _mod_specs(n_ctx_row, d):
    return [pl.BlockSpec((1, N_MOD, d), lambda b, i: (b, 0, 0)),
            pl.BlockSpec((1, N_MOD, d), lambda b, i: (n_ctx_row, 0, 0))]


def _ffn_tile(x, is_ctx, ml_ref, mc_ref, gpre_ref, gpost_ref, wg_ref, wu_ref, wd_ref, slot, fc):
    shift, scale, gate = _mod_rows(ml_ref, mc_ref, slot, is_ctx)
    h = (_rms(x, gpre_ref[...]) * (1.0 + scale) + shift).astype(BF16)
    y = jnp.zeros(x.shape, F32)
    for j in range(wg_ref.shape[1] // fc):
        cols = pl.ds(j * fc, fc)
        act = _silu(_dot(h, wg_ref[:, cols])) * _dot(h, wu_ref[:, cols])
        y = y + _dot(act.astype(BF16), wd_ref[cols, :])
    return x + 0.5 * gate * _rms(y, gpost_ref[...])


def _ffn_kernel(x_ref, ml_ref, mc_ref, gpre_ref, gpost_ref, wg_ref, wu_ref, wd_ref, o_ref,
                *, slot, n_ctx, fc):
    is_ctx = _is_ctx_rows(x_ref.shape[1], n_ctx)
    o_ref[0] = _ffn_tile(x_ref[0], is_ctx, ml_ref, mc_ref, gpre_ref, gpost_ref, wg_ref, wu_ref,
                         wd_ref, slot, fc)


def _ffn_weight_specs(d, dff, which):
    pick = lambda r, c: pl.BlockSpec((None, None, r, c), lambda *_: (*which, 0, 0),
                                     pipeline_mode=pl.Buffered(1))
    return [_const_spec((1, d)), _const_spec((1, d)), pick(d, dff), pick(d, dff), pick(dff, d)]


def ffn_call(x, mod, gpre, gpost, wg, wu, wd, *, which, slot, n_ctx, tm):
    b, t, d = x.shape
    dff = wg.shape[-1]
    kern = functools.partial(_ffn_kernel, slot=slot, n_ctx=n_ctx, fc=_pick(dff, (256, 128)))
    tok = pl.BlockSpec((1, tm, d), lambda b, i: (b, i, 0))
    return pl.pallas_call(
        kern,
        grid=(b, t // tm),
        in_specs=[tok] + _mod_specs(b, d) + _ffn_weight_specs(d, dff, which),
        out_specs=tok,
        out_shape=jax.ShapeDtypeStruct(x.shape, F32),
        compiler_params=_params(("arbitrary", "arbitrary")),
        name="ffn",
    )(x, mod, mod, gpre, gpost, wg, wu, wd)


def _mixffn_kernel(x_ref, og_ref, om_ref, ml_ref, mc_ref, gmix_ref, wog_ref, wom_ref,
                   gpre_ref, gpost_ref, wg_ref, wu_ref, wd_ref, o_ref, *, n_ctx, fc):
    is_ctx = _is_ctx_rows(x_ref.shape[1], n_ctx)
    _, _, gate = _mod_rows(ml_ref, mc_ref, 1, is_ctx)
    y = _dot(og_ref[0], wog_ref[...]) + _dot(om_ref[0], wom_ref[...])
    x = x_ref[0] + gate * _rms(y, gmix_ref[...])
    o_ref[0] = _ffn_tile(x, is_ctx, ml_ref, mc_ref, gpre_ref, gpost_ref, wg_ref, wu_ref, wd_ref, 2,
                         fc)


def mixffn_call(x, o_gdn, o_mla, mod, gmix, w_gdn, w_mla, gpre, gpost, wg, wu, wd, *, which, n_ctx,
                tm):
    b, t, d = x.shape
    dff = wg.shape[-1]
    tok = lambda w: pl.BlockSpec((1, tm, w), lambda b, i: (b, i, 0))
    kern = functools.partial(_mixffn_kernel, n_ctx=n_ctx, fc=_pick(dff, (256, 128)))
    return pl.pallas_call(
        kern,
        grid=(b, t // tm),
        in_specs=[tok(d), tok(o_gdn.shape[2]), tok(o_mla.shape[2])] + _mod_specs(b, d)
        + [_const_spec((1, d)), _const_spec(w_gdn.shape), _const_spec(w_mla.shape)]
        + _ffn_weight_specs(d, dff, which),
        out_specs=tok(d),
        out_shape=jax.ShapeDtypeStruct(x.shape, F32),
        compiler_params=_params(("arbitrary", "arbitrary")),
        name="mixffn",
    )(x, o_gdn, o_mla, mod, mod, gmix, w_gdn, w_mla, gpre, gpost, wg, wu, wd)


def _rope(x, c_ref, s1_ref, s2_ref):
    return (x * c_ref[...] + pltpu.roll(x, ROPE_HALF, 1) * s1_ref[...]
            + pltpu.roll(x, LANES - ROPE_HALF, 1) * s2_ref[...])


def _mixin_kernel(x_ref, ml_ref, mc_ref, gpre_ref, win_ref, qn_ref, kvn_ref, wuq_ref, wuk_ref,
                  wuvt_ref, gpar_ref, c_ref, s1_ref, s2_ref,
                  qkvz_ref, small_ref, q_ref, k_ref, vt_ref, *, n_ctx):
    tm = x_ref.shape[1]
    shift, scale, _ = _mod_rows(ml_ref, mc_ref, 1, _is_ctx_rows(tm, n_ctx))
    h = (_rms(x_ref[0], gpre_ref[...]) * (1.0 + scale) + shift).astype(BF16)
    n_gdn = qkvz_ref.shape[2]
    qkvz_ref[0] = _dot(h, win_ref[:, pl.ds(0, n_gdn)]).astype(BF16)
    rest = _dot(h, win_ref[:, pl.ds(n_gdn, Q_RANK + KV_RANK + LANES)])
    c_q = rest[:, :Q_RANK]
    c_kv = rest[:, Q_RANK:Q_RANK + KV_RANK]
    small = _rope(rest[:, Q_RANK + KV_RANK:], c_ref, s1_ref, s2_ref)
    lane = lax.broadcasted_iota(jnp.int32, (1, LANES), 1)
    pre = small + gpar_ref[pl.ds(1, 1), :]
    softplus = jnp.maximum(pre, 0.0) + jnp.log(1.0 + jnp.exp(-jnp.abs(pre)))
    log_decay = -jnp.exp(gpar_ref[pl.ds(0, 1), :]) * softplus
    n_dec = 2 * GDN_HEADS
    is_dec = (lane >= GATE_LANE0) & (lane < GATE_LANE0 + n_dec)
    is_beta = (lane >= GATE_LANE0 + n_dec) & (lane < GATE_LANE0 + 2 * n_dec)
    small_ref[0] = jnp.where(is_dec, log_decay, jnp.where(is_beta, _sigmoid(small), small))
    k_rope = jnp.where((lane >= ROPE_LANE0) & (lane < GATE_LANE0), small, 0.0)

    q = _dot(_rms(c_q, qn_ref[...]).astype(BF16), wuq_ref[...]) * (MLA_SCALE * LOG2_E)
    ckv_n = _rms(c_kv, kvn_ref[...]).astype(BF16)
    k = _dot(ckv_n, wuk_ref[...])
    for hd in range(MLA_HEADS):
        cols = pl.ds(hd * LANES, LANES)
        blk = slice(hd * LANES, (hd + 1) * LANES)
        q_ref[0, :, cols] = _rope(q[:, blk], c_ref, s1_ref, s2_ref).astype(BF16)
        k_ref[0, :, cols] = (k[:, blk] + k_rope).astype(BF16)
    vt_ref[0] = _dot_nt(wuvt_ref[...], ckv_n).astype(BF16)


def mixin_call(x, mod, gpre, win, qn, kvn, wuq, wuk, wuv_t, gpar, rope_c, rope_s1, rope_s2, *, n_ctx,
               tm):
    b, t, d = x.shape
    n_gdn = 4 * GDN_WIDTH
    hw = MLA_HEADS * LANES
    vrows = wuv_t.shape[0]
    tok = lambda w: pl.BlockSpec((1, tm, w), lambda b, i: (b, i, 0))
    tab = pl.BlockSpec((tm, LANES), lambda b, i: (i, 0))
    kern = functools.partial(_mixin_kernel, n_ctx=n_ctx)
    return pl.pallas_call(
        kern,
        grid=(b, t // tm),
        in_specs=[tok(d)] + _mod_specs(b, d)
        + [_const_spec((1, d)), _const_spec(win.shape), _const_spec((1, Q_RANK)),
           _const_spec((1, KV_RANK)), _const_spec(wuq.shape), _const_spec(wuk.shape),
           _const_spec(wuv_t.shape), _const_spec(gpar.shape), tab, tab, tab],
        out_specs=[tok(n_gdn), tok(LANES), tok(hw), tok(hw),
                   pl.BlockSpec((1, vrows, tm), lambda b, i: (b, 0, i))],
        out_shape=[jax.ShapeDtypeStruct((b, t, n_gdn), BF16),
                   jax.ShapeDtypeStruct((b, t, LANES), F32),
                   jax.ShapeDtypeStruct((b, t, hw), BF16),
                   jax.ShapeDtypeStruct((b, t, hw), BF16),
                   jax.ShapeDtypeStruct((b, vrows, t), BF16)],
        compiler_params=_params(("arbitrary", "arbitrary")),
        name="mixin",
    )(x, mod, mod, gpre, win, qn, kvn, wuq, wuk, wuv_t, gpar, rope_c, rope_s1, rope_s2)


def _level_masks(ck):
    ii = lax.broadcasted_iota(jnp.int32, (ck, ck), 0)
    jj = lax.broadcasted_iota(jnp.int32, (ck, ck), 1)
    masks = [ii == jj]
    for lvl in range(1, ck.bit_length()):
        masks.append(((ii >> lvl) == (jj >> lvl)) & ((ii >> (lvl - 1)) != (jj >> (lvl - 1))))
    return jnp.stack(masks).astype(BF16)


def _gdn_kernel(q_ref, k_ref, v_ref, z_ref, cq_ref, ck_ref, cv_ref, gate_ref, lvl_ref, onorm_ref,
                o_ref, a_s, rhs_s, kdt_s, u0_s, w_s, qd_s, qk_s, n_s, m_s, dec_s, st_s,
                *, n_ctx, ck, unroll):
    t = q_ref.shape[1]
    dk = q_ref.shape[2]
    nc = t // ck
    nl = n_ctx // ck
    n_lvl = lvl_ref.shape[0] - 1
    halo = BF16_ROWS

    ii = lax.broadcasted_iota(jnp.int32, (ck, ck), 0)
    jj = lax.broadcasted_iota(jnp.int32, (ck, ck), 1)
    eye = ii == jj

    def conv(u_ref, w_ref, c):
        r0 = c * ck
        keep_lo = 0.0 if c in (0, nl) else 1.0
        keep_hi = 0.0 if c in (nl - 1, nc - 1) else 1.0
        full = jnp.concatenate(
            [u_ref[0, pl.ds(max(r0 - halo, 0), halo), :].astype(F32) * keep_lo,
             u_ref[0, pl.ds(r0, ck), :].astype(F32),
             u_ref[0, pl.ds(min(r0 + ck, t - halo), halo), :].astype(F32) * keep_hi], axis=0)
        acc = full[halo:halo + ck] * w_ref[pl.ds(CONV_PAD, 1), :]
        for s in range(-CONV_PAD, CONV_PAD + 1):
            if s != 0:
                acc = acc + full[halo + s:halo + s + ck] * w_ref[pl.ds(CONV_PAD + s, 1), :]
        return _silu(acc)

    def l2n(u):
        return u * lax.rsqrt(jnp.sum(u * u, axis=-1, keepdims=True) + EPS)

    def stage_a(c):
        box = {}

        def shared():
            box["q"] = l2n(conv(q_ref, cq_ref, c)) * (dk ** -0.5)
            box["k"] = l2n(conv(k_ref, ck_ref, c))
            box["v"] = conv(v_ref, cv_ref, c)
            box["kb"] = box["k"].astype(BF16)
            box["qk_raw"] = _dot_nt(box["q"].astype(BF16), box["kb"])
            box["k_t"] = box["k"].T

        def direction(d):
            q, k, v, kb = box["q"], box["k"], box["v"], box["kb"]
            gb = gate_ref[0, 0, pl.ds(c * ck, ck), :]
            incl = (ii >= jj) if d == 0 else (ii <= jj)
            strict = (ii > jj) if d == 0 else (ii < jj)
            incl_t = (ii <= jj) if d == 0 else (ii >= jj)
            g_col = gb[:, d:d + 1]
            beta = gb[:, 2 + d:3 + d]
            g_row = jnp.sum(jnp.where(eye, g_col, 0.0), axis=0, keepdims=True)
            cum_col = jnp.sum(jnp.where(incl, g_row, 0.0), axis=1, keepdims=True)
            cum_row = jnp.sum(jnp.where(incl_t, g_col, 0.0), axis=0, keepdims=True)
            total = jnp.sum(g_col, axis=0, keepdims=True)
            e = jnp.exp(cum_col - cum_row)
            kk = _dot_nt((beta * k).astype(BF16), kb)
            a_s[d, c] = (jnp.where(strict, e, 0.0) * kk).astype(BF16)
            qk_s[d, c] = (jnp.where(incl, e, 0.0) * box["qk_raw"]).astype(BF16)
            e_cum = jnp.exp(cum_col)
            rhs_s[d, c] = jnp.concatenate([beta * v, (beta * e_cum) * k], axis=1).astype(BF16)
            qd_s[d, c] = (q * e_cum).astype(BF16)
            dec_s[d, c] = jnp.broadcast_to(jnp.exp(total), (1, dk))
            kdt_s[d, c] = (box["k_t"] * jnp.exp(total - cum_row)).astype(BF16)

        return [shared, functools.partial(direction, 0), functools.partial(direction, 1)]

    def stage_b(chunks, fillers):
        chains = [(d, c) for c in chunks for d in range(2)]
        x = {ch: lvl_ref[0] - a_s[ch] * lvl_ref[1] for ch in chains}
        levels = range(2, n_lvl + 1)
        for i, lvl in enumerate(levels):
            xa = {ch: _dot(x[ch], a_s[ch]).astype(BF16) for ch in chains}
            ys = {ch: _dot(xa[ch], x[ch]) for ch in chains}
            for ch in chains:
                x[ch] = x[ch] - ys[ch].astype(BF16) * lvl_ref[lvl]
            for thunk in fillers[len(fillers) * i // len(levels):
                                 len(fillers) * (i + 1) // len(levels)]:
                thunk()
        for ch in chains:
            sol = _dot(x[ch], rhs_s[ch])
            u0_s[ch] = sol[:, :dk]
            w_s[ch] = sol[:, dk:].astype(BF16)
            nm = _dot(kdt_s[ch], sol.astype(BF16))
            n_s[ch] = nm[:, :dk]
            m_s[ch] = nm[:, dk:].astype(BF16)

    groups = [range(g * unroll, (g + 1) * unroll) for g in range(nc // unroll)]
    pieces = [[thunk for c in grp for thunk in stage_a(c)] for grp in groups]
    for thunk in pieces[0]:
        thunk()
    for g, grp in enumerate(groups):
        stage_b(grp, pieces[g + 1] if g + 1 < len(groups) else [])

    def scan_body(i, carry):
        c_b = jnp.where(i < nl, nl - 1 - i, nc - 1 + nl - i)
        new = []
        for d, c, s in ((0, i, carry[0]), (1, c_b, carry[1])):
            sb = s.astype(BF16)
            st_s[d, c] = sb
            new.append(dec_s[d, c] * s + n_s[d, c] - _dot(m_s[d, c], sb))
        return tuple(new)

    zero = jnp.zeros((dk, dk), F32)
    lax.fori_loop(0, nc, scan_body, (zero, zero))

    def out_body(gi, carry):
        for u in range(unroll):
            c = gi * unroll + u
            rows = pl.ds(pl.multiple_of(c * ck, ck), ck)
            o = jnp.zeros((ck, dk), F32)
            for d in range(2):
                sb = st_s[d, c]
                us = u0_s[d, c] - _dot(w_s[d, c], sb)
                o = o + _dot(qd_s[d, c], sb) + _dot(qk_s[d, c], us.astype(BF16))
            z = z_ref[0, rows, :].astype(F32)
            o_ref[0, rows, :] = (_rms(o, onorm_ref[...]) * _silu(z)).astype(BF16)
        return carry

    lax.fori_loop(0, nc // unroll, out_body, 0)


def gdn_call(qkvz, gates, conv_w, out_norm, *, n_ctx, ck):
    b, t, _ = qkvz.shape
    h, dk = GDN_HEADS, GDN_HEAD_DIM
    nc = t // ck
    unroll = _pick(nc, (3, 2, 1))
    lvl = _level_masks(ck)
    col = lambda off: pl.BlockSpec((1, t, dk), lambda b, hd: (b, 0, off + hd))
    cw = lambda off: pl.BlockSpec((CONV_K, dk), lambda b, hd: (0, off + hd))
    both = lambda r, w, dt: pltpu.VMEM((2, nc, r, w), dt)
    kern = functools.partial(_gdn_kernel, n_ctx=n_ctx, ck=ck, unroll=unroll)
    return pl.pallas_call(
        kern,
        grid=(b, h),
        in_specs=[col(0), col(h), col(2 * h), col(3 * h), cw(0), cw(h), cw(2 * h),
                  pl.BlockSpec((1, 1, t, 4), lambda b, hd: (b, hd, 0, 0)),
                  _const_spec(lvl.shape),
                  pl.BlockSpec((1, dk), lambda b, hd: (0, 0))],
        out_specs=pl.BlockSpec((1, t, dk), lambda b, hd: (b, 0, hd)),
        out_shape=jax.ShapeDtypeStruct((b, t, h * dk), BF16),
        scratch_shapes=[both(ck, ck, BF16), both(ck, 2 * dk, BF16), both(dk, ck, BF16),
                        both(ck, dk, F32), both(ck, dk, BF16), both(ck, dk, BF16),
                        both(ck, ck, BF16), both(dk, dk, F32), both(dk, dk, BF16),
                        both(1, dk, F32), both(dk, dk, BF16)],
        compiler_params=_params(("arbitrary", "arbitrary")),
        name="gdn",
    )(qkvz, qkvz, qkvz, qkvz, conv_w, conv_w, conv_w, gates, lvl, out_norm)


def _attn_kernel(q_ref, k_ref, vt_ref, o_ref, *, n_ctx):
    tq = q_ref.shape[1]
    t = k_ref.shape[1]

    def attend(n_keys):
        keys = pl.ds(0, n_keys)
        outs = []
        for g0 in range(0, MLA_HEADS, HEAD_GROUP):
            heads = range(g0, g0 + HEAD_GROUP)
            s_ts = [_dot_nt(k_ref[0, keys, pl.ds(hd * LANES, LANES)],
                            q_ref[0, :, pl.ds(hd * LANES, LANES)]) for hd in heads]
            p_ts = [jnp.exp2(s_t - jnp.max(s_t, axis=0, keepdims=True)) for s_t in s_ts]
            denoms = [jnp.sum(p_t, axis=0, keepdims=True) for p_t in p_ts]
            for hd, p_t, denom in zip(heads, p_ts, denoms):
                o_t = _dot(vt_ref[0, pl.ds(hd * MLA_V, MLA_V), keys], p_t.astype(BF16))
                outs.append(o_t / denom)
        o_ref[0] = jnp.concatenate(outs, axis=0).T.astype(BF16)

    is_ctx = pl.program_id(1) * tq < n_ctx

    @pl.when(is_ctx)
    def _():
        attend(n_ctx)

    @pl.when(jnp.logical_not(is_ctx))
    def _():
        attend(t)


def attn_call(q, k, v_t, *, n_ctx, tq):
    b, t, hw = q.shape
    vw = MLA_HEADS * MLA_V
    kern = functools.partial(_attn_kernel, n_ctx=n_ctx)
    return pl.pallas_call(
        kern,
        grid=(b, t // tq),
        in_specs=[pl.BlockSpec((1, tq, hw), lambda b, i: (b, i, 0)),
                  pl.BlockSpec((1, t, hw), lambda b, i: (b, 0, 0)),
                  pl.BlockSpec((1, v_t.shape[1], t), lambda b, i: (b, 0, 0))],
        out_specs=pl.BlockSpec((1, tq, vw), lambda b, i: (b, i, 0)),
        out_shape=jax.ShapeDtypeStruct((b, t, vw), BF16),
        compiler_params=_params(("arbitrary", "arbitrary")),
        name="attn",
    )(q, k, v_t)


def _head_pad(w, width):
    kdim = w.shape[0]
    w = w.reshape(kdim, MLA_HEADS, width)
    return jnp.pad(w, ((0, 0), (0, 0), (0, LANES - width))).reshape(kdim, MLA_HEADS * LANES)


def _layout_w_in(w):
    d = w.shape[0]
    off_a = 4 * GDN_WIDTH
    off_b = off_a + 2 * GDN_HEADS
    off_cq = off_b + 2 * GDN_HEADS
    off_ckv = off_cq + Q_RANK
    off_kr = off_ckv + KV_RANK
    small = jnp.concatenate(
        [jnp.zeros((d, ROPE_LANE0), w.dtype), w[:, off_kr:off_kr + MLA_ROPE], w[:, off_a:off_cq],
         jnp.zeros((d, LANES - GATE_LANE0 - 4 * GDN_HEADS), w.dtype)], axis=1)
    return jnp.concatenate([w[:, :off_a], w[:, off_cq:off_kr], small], axis=1).astype(BF16)


def _rope_tables(n_ctx, seq):
    rows = seq // GRID_W
    row = jnp.repeat(jnp.arange(rows), GRID_W).astype(F32)
    col = jnp.tile(jnp.arange(GRID_W), rows).astype(F32)
    axis_dim = MLA_ROPE // 2
    inv_freq = jnp.power(ROPE_BASE, -jnp.arange(0, axis_dim, 2, dtype=F32) / axis_dim)
    ang = jnp.concatenate([row[:, None] * inv_freq, col[:, None] * inv_freq], axis=-1)
    cos, sin = jnp.cos(ang), jnp.sin(ang)
    ones = lambda n: jnp.ones((seq, n), F32)
    zeros = lambda n: jnp.zeros((seq, n), F32)
    tail = LANES - GATE_LANE0
    c = jnp.concatenate([ones(ROPE_LANE0), cos, cos, ones(tail)], axis=1)
    s1 = jnp.concatenate([zeros(ROPE_LANE0 + ROPE_HALF), sin, zeros(tail)], axis=1)
    s2 = jnp.concatenate([zeros(ROPE_LANE0), -sin, zeros(ROPE_HALF + tail)], axis=1)
    ctx = lambda fill: jnp.full((n_ctx, LANES), fill, F32)
    return (jnp.concatenate([ctx(1.0), c]), jnp.concatenate([ctx(0.0), s1]),
            jnp.concatenate([ctx(0.0), s2]))


def kernel(x, c, ctx, c_ctx, w_ada, b_ada, norm_pre, norm_post, ffn_w_gate, ffn_w_up, ffn_w_down,
           w_in, gdn_conv, gdn_a_log, gdn_dt_bias, gdn_out_norm, mla_q_norm, mla_kv_norm, mla_w_uq,
           mla_w_ukv, w_out):
    bsz, seq, d = x.shape
    n_ctx = ctx.shape[1]
    t = n_ctx + seq
    depth = w_ada.shape[0]
    tm = _pick(t, (768, 512, 384, 256, 128))
    tq = _pick(n_ctx, (256, 128))
    ck = tq
    assert seq % ck == 0 and seq % GRID_W == 0

    cond_rows = -(-(bsz + 1) // SUBLANES) * SUBLANES
    cond = jnp.concatenate([c, c_ctx[None], jnp.zeros((cond_rows - bsz - 1, d), F32)], axis=0)
    mod = ada_call(cond, w_ada, b_ada).reshape(depth, cond_rows, N_MOD, d)
    rope_c, rope_s1, rope_s2 = _rope_tables(n_ctx, seq)

    ffn_w = (ffn_w_gate.astype(BF16), ffn_w_up.astype(BF16), ffn_w_down.astype(BF16))
    xs = jnp.concatenate([ctx, x], axis=1)
    for l in range(depth):
        row = lambda v: v.reshape(1, -1)
        xs = ffn_call(xs, mod[l], row(norm_pre[l, 0]), row(norm_post[l, 0]), *ffn_w,
                      which=(l, 0), slot=0, n_ctx=n_ctx, tm=tm)

        uq = mla_w_uq[l].reshape(Q_RANK, MLA_HEADS, MLA_NOPE + MLA_ROPE)
        ukv = mla_w_ukv[l].reshape(KV_RANK, MLA_HEADS, MLA_NOPE + MLA_V)
        wuq = _head_pad(uq.reshape(Q_RANK, -1), MLA_NOPE + MLA_ROPE).astype(BF16)
        wuk = _head_pad(ukv[:, :, :MLA_NOPE].reshape(KV_RANK, -1), MLA_NOPE).astype(BF16)
        wuv_t = ukv[:, :, MLA_NOPE:].reshape(KV_RANK, MLA_HEADS * MLA_V).T.astype(BF16)
        gpar = jnp.pad(jnp.stack([gdn_a_log[l].reshape(-1), gdn_dt_bias[l].reshape(-1)]),
                       ((0, 0), (GATE_LANE0, LANES - GATE_LANE0 - 2 * GDN_HEADS)))
        qkvz, small, q, k, v_t = mixin_call(
            xs, mod[l], row(norm_pre[l, 1]), _layout_w_in(w_in[l]), row(mla_q_norm[l]),
            row(mla_kv_norm[l]), wuq, wuk, wuv_t, gpar, rope_c, rope_s1, rope_s2, n_ctx=n_ctx,
            tm=tm)

        ab = small[:, :, GATE_LANE0:GATE_LANE0 + 4 * GDN_HEADS]
        gates = ab.reshape(bsz, t, 4, GDN_HEADS).transpose(0, 3, 1, 2)
        o_gdn = gdn_call(qkvz, gates, gdn_conv[l], row(gdn_out_norm[l]), n_ctx=n_ctx, ck=ck)
        o_mla = attn_call(q, k, v_t, n_ctx=n_ctx, tq=tq)
        xs = mixffn_call(xs, o_gdn, o_mla, mod[l], row(norm_post[l, 1]),
                         w_out[l, :GDN_WIDTH].astype(BF16), w_out[l, GDN_WIDTH:].astype(BF16),
                         row(norm_pre[l, 2]), row(norm_post[l, 2]), *ffn_w, which=(l, 1),
                         n_ctx=n_ctx, tm=tm)
    return xs[:, n_ctx:]
```

```python
import functools
import math

import jax
import jax.numpy as jnp
from jax import lax
from jax.experimental import pallas as pl
from jax.experimental.pallas import tpu as pltpu

F32 = jnp.float32
BF16 = jnp.bfloat16

EPS = 1e-6
N_MOD = 9
GDN_HEADS = 4
GDN_HEAD_DIM = 128
GDN_WIDTH = GDN_HEADS * GDN_HEAD_DIM
CONV_K = 5
CONV_PAD = CONV_K // 2
MLA_HEADS = 8
MLA_NOPE = 64
MLA_ROPE = 32
MLA_V = 64
Q_RANK = 384
KV_RANK = 256
ROPE_BASE = 10000.0
GRID_W = 64
MLA_SCALE = (MLA_NOPE + MLA_ROPE) ** -0.5
LOG2_E = math.log2(math.e)

LANES = 128
SUBLANES = 8
ROPE_HALF = MLA_ROPE // 2
ROPE_LANE0 = MLA_NOPE
GATE_LANE0 = ROPE_LANE0 + MLA_ROPE
HEAD_GROUP = 4
BF16_ROWS = 16
VMEM_LIMIT = 56 * 1024 * 1024


def _pick(n, candidates):
    for cand in candidates:
        if n % cand == 0:
            return cand
    raise ValueError(f"no tile for {n}")


def _sigmoid(x):
    return 1.0 / (1.0 + jnp.exp(-x))


def _silu(x):
    return x * _sigmoid(x)


def _rms(x, gain):
    return x * lax.rsqrt(jnp.mean(x * x, axis=-1, keepdims=True) + EPS) * gain


def _dot(a, b):
    return jnp.dot(a, b, preferred_element_type=F32)


def _dot_nt(a, b):
    return lax.dot_general(a, b, (((1,), (1,)), ((), ())), preferred_element_type=F32)


def _const_spec(shape):
    zeros = (0,) * len(shape)
    return pl.BlockSpec(shape, lambda *_: zeros, pipeline_mode=pl.Buffered(1))


def _params(sem):
    return pltpu.CompilerParams(dimension_semantics=sem, vmem_limit_bytes=VMEM_LIMIT)


def _ada_kernel(s_ref, w_ref, b_ref, o_ref):
    s = _silu(s_ref[...])
    o_ref[0] = jnp.dot(s, w_ref[0], precision=lax.Precision.HIGHEST,
                       preferred_element_type=F32) + b_ref[0]


def ada_call(cond, w_ada, b_ada):
    depth, d, n = w_ada.shape
    rows = cond.shape[0]
    tn = _pick(n, (2304, 1024, 512, 256, 128))
    return pl.pallas_call(
        _ada_kernel,
        grid=(depth, n // tn),
        in_specs=[pl.BlockSpec((rows, d), lambda l, j: (0, 0)),
                  pl.BlockSpec((1, d, tn), lambda l, j: (l, 0, j)),
                  pl.BlockSpec((1, 1, tn), lambda l, j: (l, 0, j))],
        out_specs=pl.BlockSpec((1, rows, tn), lambda l, j: (l, 0, j)),
        out_shape=jax.ShapeDtypeStruct((depth, rows, n), F32),
        compiler_params=_params(("arbitrary", "arbitrary")),
        name="ada",
    )(cond, w_ada, b_ada.reshape(depth, 1, n))


def _mod_vec(ml_ref, mc_ref, slot, j):
    return ml_ref[0, pl.ds(3 * slot + j, 1), :], mc_ref[0, pl.ds(3 * slot + j, 1), :]


def _per_row(is_ctx, lat_ctx):
    return jnp.where(is_ctx, lat_ctx[1], lat_ctx[0])


def _pre_mod(x, is_ctx, ml_ref, mc_ref, slot, gain):
    inv = lax.rsqrt(jnp.mean(x * x, axis=-1, keepdims=True) + EPS)
    mult = tuple(gain * (1.0 + s) for s in _mod_vec(ml_ref, mc_ref, slot, 1))
    shift = _mod_vec(ml_ref, mc_ref, slot, 0)
    return (x * inv) * _per_row(is_ctx, mult) + _per_row(is_ctx, shift)


def _post_res(x, y, is_ctx, ml_ref, mc_ref, slot, gain, weight):
    inv = lax.rsqrt(jnp.mean(y * y, axis=-1, keepdims=True) + EPS)
    mult = tuple((weight * gain) * g for g in _mod_vec(ml_ref, mc_ref, slot, 2))
    return x + (y * inv) * _per_row(is_ctx, mult)


def _is_ctx_rows(tm, n_ctx):
    rows = pl.program_id(1) * tm + lax.broadcasted_iota(jnp.int32, (tm, 1), 0)
    return rows < n_ctx


def _mod_specs(n_ctx_row, d):
    return [pl.BlockSpec((1, N_MOD, d), lambda b, i: (b, 0, 0)),
            pl.BlockSpec((1, N_MOD, d), lambda b, i: (n_ctx_row, 0, 0))]


def _ffn_tile(x, is_ctx, ml_ref, mc_ref, gpre_ref, gpost_ref, wg_ref, wu_ref, wd_ref, slot, fc):
    h = _pre_mod(x, is_ctx, ml_ref, mc_ref, slot, gpre_ref[...]).astype(BF16)
    y = jnp.zeros(x.shape, F32)
    for j in range(wg_ref.shape[1] // fc):
        cols = pl.ds(j * fc, fc)
        act = _silu(_dot(h, wg_ref[:, cols])) * _dot(h, wu_ref[:, cols])
        y = y + _dot(act.astype(BF16), wd_ref[cols, :])
    return _post_res(x, y, is_ctx, ml_ref, mc_ref, slot, gpost_ref[...], 0.5)


def _ffn_kernel(x_ref, ml_ref, mc_ref, gpre_ref, gpost_ref, wg_ref, wu_ref, wd_ref, o_ref,
                *, slot, n_ctx, fc):
    is_ctx = _is_ctx_rows(x_ref.shape[1], n_ctx)
    o_ref[0] = _ffn_tile(x_ref[0], is_ctx, ml_ref, mc_ref, gpre_ref, gpost_ref, wg_ref, wu_ref,
                         wd_ref, slot, fc)


def _ffn_weight_specs(d, dff, which):
    pick = lambda r, c: pl.BlockSpec((None, None, r, c), lambda *_: (*which, 0, 0),
                                     pipeline_mode=pl.Buffered(1))
    return [_const_spec((1, d)), _const_spec((1, d)), pick(d, dff), pick(d, dff), pick(dff, d)]


def ffn_call(x, mod, gpre, gpost, wg, wu, wd, *, which, slot, n_ctx, tm):
    b, t, d = x.shape
    dff = wg.shape[-1]
    kern = functools.partial(_ffn_kernel, slot=slot, n_ctx=n_ctx, fc=_pick(dff, (256, 128)))
    tok = pl.BlockSpec((1, tm, d), lambda b, i: (b, i, 0))
    return pl.pallas_call(
        kern,
        grid=(b, t // tm),
        in_specs=[tok] + _mod_specs(b, d) + _ffn_weight_specs(d, dff, which),
        out_specs=tok,
        out_shape=jax.ShapeDtypeStruct(x.shape, F32),
        compiler_params=_params(("arbitrary", "arbitrary")),
        name="ffn",
    )(x, mod, mod, gpre, gpost, wg, wu, wd)


def _mixffn_kernel(x_ref, og_ref, om_ref, ml_ref, mc_ref, gmix_ref, wog_ref, wom_ref,
                   gpre_ref, gpost_ref, wg_ref, wu_ref, wd_ref, o_ref, *, n_ctx, fc):
    is_ctx = _is_ctx_rows(x_ref.shape[1], n_ctx)
    y = _dot(og_ref[0], wog_ref[...]) + _dot(om_ref[0], wom_ref[...])
    x = _post_res(x_ref[0], y, is_ctx, ml_ref, mc_ref, 1, gmix_ref[...], 1.0)
    o_ref[0] = _ffn_tile(x, is_ctx, ml_ref, mc_ref, gpre_ref, gpost_ref, wg_ref, wu_ref, wd_ref, 2,
                         fc)


def mixffn_call(x, o_gdn, o_mla, mod, gmix, w_gdn, w_mla, gpre, gpost, wg, wu, wd, *, which, n_ctx,
                tm):
    b, t, d = x.shape
    dff = wg.shape[-1]
    tok = lambda w: pl.BlockSpec((1, tm, w), lambda b, i: (b, i, 0))
    kern = functools.partial(_mixffn_kernel, n_ctx=n_ctx, fc=_pick(dff, (256, 128)))
    return pl.pallas_call(
        kern,
        grid=(b, t // tm),
        in_specs=[tok(d), tok(o_gdn.shape[2]), tok(o_mla.shape[2])] + _mod_specs(b, d)
        + [_const_spec((1, d)), _const_spec(w_gdn.shape), _const_spec(w_mla.shape)]
        + _ffn_weight_specs(d, dff, which),
        out_specs=tok(d),
        out_shape=jax.ShapeDtypeStruct(x.shape, F32),
        compiler_params=_params(("arbitrary", "arbitrary")),
        name="mixffn",
    )(x, o_gdn, o_mla, mod, mod, gmix, w_gdn, w_mla, gpre, gpost, wg, wu, wd)


def _rope(x, c_ref, s1_ref, s2_ref):
    return (x * c_ref[...] + pltpu.roll(x, ROPE_HALF, 1) * s1_ref[...]
            + pltpu.roll(x, LANES - ROPE_HALF, 1) * s2_ref[...])


def _mixin_kernel(x_ref, ml_ref, mc_ref, gpre_ref, win_ref, qn_ref, kvn_ref, wuq_ref, wuk_ref,
                  wuvt_ref, gpar_ref, c_ref, s1_ref, s2_ref,
                  qkvz_ref, small_ref, q_ref, k_ref, vt_ref, *, n_ctx):
    tm = x_ref.shape[1]
    h = _pre_mod(x_ref[0], _is_ctx_rows(tm, n_ctx), ml_ref, mc_ref, 1, gpre_ref[...]).astype(BF16)
    n_gdn = qkvz_ref.shape[2]
    qkvz_ref[0] = _dot(h, win_ref[:, pl.ds(0, n_gdn)]).astype(BF16)
    rest = _dot(h, win_ref[:, pl.ds(n_gdn, Q_RANK + KV_RANK + LANES)])
    c_q = rest[:, :Q_RANK]
    c_kv = rest[:, Q_RANK:Q_RANK + KV_RANK]
    small = _rope(rest[:, Q_RANK + KV_RANK:], c_ref, s1_ref, s2_ref)
    lane = lax.broadcasted_iota(jnp.int32, (1, LANES), 1)
    pre = small + gpar_ref[pl.ds(1, 1), :]
    softplus = jnp.maximum(pre, 0.0) + jnp.log(1.0 + jnp.exp(-jnp.abs(pre)))
    log_decay = -jnp.exp(gpar_ref[pl.ds(0, 1), :]) * softplus
    n_dec = 2 * GDN_HEADS
    is_dec = (lane >= GATE_LANE0) & (lane < GATE_LANE0 + n_dec)
    is_beta = (lane >= GATE_LANE0 + n_dec) & (lane < GATE_LANE0 + 2 * n_dec)
    small_ref[0] = jnp.where(is_dec, log_decay, jnp.where(is_beta, _sigmoid(small), small))
    k_rope = jnp.where((lane >= ROPE_LANE0) & (lane < GATE_LANE0), small, 0.0)

    q = _dot(_rms(c_q, qn_ref[...]).astype(BF16), wuq_ref[...]) * (MLA_SCALE * LOG2_E)
    ckv_n = _rms(c_kv, kvn_ref[...]).astype(BF16)
    k = _dot(ckv_n, wuk_ref[...])
    for hd in range(MLA_HEADS):
        cols = pl.ds(hd * LANES, LANES)
        blk = slice(hd * LANES, (hd + 1) * LANES)
        q_ref[0, :, cols] = _rope(q[:, blk], c_ref, s1_ref, s2_ref).astype(BF16)
        k_ref[0, :, cols] = (k[:, blk] + k_rope).astype(BF16)
    vt_ref[0] = _dot_nt(wuvt_ref[...], ckv_n).astype(BF16)


def mixin_call(x, mod, gpre, win, qn, kvn, wuq, wuk, wuv_t, gpar, rope_c, rope_s1, rope_s2, *, n_ctx,
               tm):
    b, t, d = x.shape
    n_gdn = 4 * GDN_WIDTH
    hw = MLA_HEADS * LANES
    vrows = wuv_t.shape[0]
    tok = lambda w: pl.BlockSpec((1, tm, w), lambda b, i: (b, i, 0))
    tab = pl.BlockSpec((tm, LANES), lambda b, i: (i, 0))
    kern = functools.partial(_mixin_kernel, n_ctx=n_ctx)
    return pl.pallas_call(
        kern,
        grid=(b, t // tm),
        in_specs=[tok(d)] + _mod_specs(b, d)
        + [_const_spec((1, d)), _const_spec(win.shape), _const_spec((1, Q_RANK)),
           _const_spec((1, KV_RANK)), _const_spec(wuq.shape), _const_spec(wuk.shape),
           _const_spec(wuv_t.shape), _const_spec(gpar.shape), tab, tab, tab],
        out_specs=[tok(n_gdn), tok(LANES), tok(hw), tok(hw),
                   pl.BlockSpec((1, vrows, tm), lambda b, i: (b, 0, i))],
        out_shape=[jax.ShapeDtypeStruct((b, t, n_gdn), BF16),
                   jax.ShapeDtypeStruct((b, t, LANES), F32),
                   jax.ShapeDtypeStruct((b, t, hw), BF16),
                   jax.ShapeDtypeStruct((b, t, hw), BF16),
                   jax.ShapeDtypeStruct((b, vrows, t), BF16)],
        compiler_params=_params(("arbitrary", "arbitrary")),
        name="mixin",
    )(x, mod, mod, gpre, win, qn, kvn, wuq, wuk, wuv_t, gpar, rope_c, rope_s1, rope_s2)


def _level_masks(ck):
    ii = lax.broadcasted_iota(jnp.int32, (ck, ck), 0)
    jj = lax.broadcasted_iota(jnp.int32, (ck, ck), 1)
    masks = [ii == jj]
    for lvl in range(1, ck.bit_length()):
        masks.append(((ii >> lvl) == (jj >> lvl)) & ((ii >> (lvl - 1)) != (jj >> (lvl - 1))))
    masks.append(ii >= jj)
    return jnp.stack(masks).astype(BF16)


def _gdn_kernel(q_ref, k_ref, v_ref, z_ref, cq_ref, ck_ref, cv_ref, gate_ref, lvl_ref, onorm_ref,
                o_ref, cum_s, cumt_s, a_s, rhs_s, kdt_s, u0_s, w_s, qd_s, qk_s, n_s, m_s, dec_s, st_s,
                *, n_ctx, ck, unroll):
    t = q_ref.shape[1]
    dk = q_ref.shape[2]
    nc = t // ck
    nl = n_ctx // ck
    n_lvl = lvl_ref.shape[0] - 2
    halo = BF16_ROWS

    ii = lax.broadcasted_iota(jnp.int32, (ck, ck), 0)
    jj = lax.broadcasted_iota(jnp.int32, (ck, ck), 1)

    gates = gate_ref[0, 0]
    part, prefix = gates, jnp.zeros(gates.shape, F32)
    for _ in range(3):
        term = part.astype(BF16)
        prefix = prefix + _dot(lvl_ref[n_lvl + 1], term)
        part = part - term.astype(F32)
    totals = prefix[ck - 1:ck, :]
    lane = lax.broadcasted_iota(jnp.int32, (1, gates.shape[1]), 1)
    cum_s[...] = jnp.where(lane < nc, prefix, totals - prefix + gates)
    cumt_s[...] = cum_s[...].T

    def conv(u_ref, w_ref, c):
        r0 = c * ck
        keep_lo = 0.0 if c in (0, nl) else 1.0
        keep_hi = 0.0 if c in (nl - 1, nc - 1) else 1.0
        full = jnp.concatenate(
            [u_ref[0, pl.ds(max(r0 - halo, 0), halo), :].astype(F32) * keep_lo,
             u_ref[0, pl.ds(r0, ck), :].astype(F32),
             u_ref[0, pl.ds(min(r0 + ck, t - halo), halo), :].astype(F32) * keep_hi], axis=0)
        acc = full[halo:halo + ck] * w_ref[pl.ds(CONV_PAD, 1), :]
        for s in range(-CONV_PAD, CONV_PAD + 1):
            if s != 0:
                acc = acc + full[halo + s:halo + s + ck] * w_ref[pl.ds(CONV_PAD + s, 1), :]
        return _silu(acc)

    def l2n(u):
        return u * lax.rsqrt(jnp.sum(u * u, axis=-1, keepdims=True) + EPS)

    def stage_a(c):
        box = {}

        def shared():
            box["q"] = l2n(conv(q_ref, cq_ref, c)) * (dk ** -0.5)
            box["k"] = l2n(conv(k_ref, ck_ref, c))
            box["v"] = conv(v_ref, cv_ref, c)
            box["kb"] = box["k"].astype(BF16)
            box["qk_raw"] = _dot_nt(box["q"].astype(BF16), box["kb"])
            box["k_t"] = box["k"].T

        def direction(d):
            q, k, v, kb = box["q"], box["k"], box["v"], box["kb"]
            incl = (ii >= jj) if d == 0 else (ii <= jj)
            strict = (ii > jj) if d == 0 else (ii < jj)
            col = d * nc + c
            beta = gate_ref[0, 0, :, pl.ds(col + 2 * nc, 1)]
            cum_col = cum_s[:, pl.ds(col, 1)]
            cum_row = cumt_s[pl.ds(col, 1), :]
            total = totals[:, col:col + 1]
            e = jnp.exp(cum_col - cum_row)
            kk = _dot_nt((beta * k).astype(BF16), kb)
            a_s[d, c] = (jnp.where(strict, e, 0.0) * kk).astype(BF16)
            qk_s[d, c] = (jnp.where(incl, e, 0.0) * box["qk_raw"]).astype(BF16)
            e_cum = jnp.exp(cum_col)
            rhs_s[d, c] = jnp.concatenate([beta * v, (beta * e_cum) * k], axis=1).astype(BF16)
            qd_s[d, c] = (q * e_cum).astype(BF16)
            dec_s[d, c] = jnp.broadcast_to(jnp.exp(total), (1, dk))
            kdt_s[d, c] = (box["k_t"] * jnp.exp(total - cum_row)).astype(BF16)

        return [shared, functools.partial(direction, 0), functools.partial(direction, 1)]

    def stage_b(chunks, fillers):
        chains = [(d, c) for c in chunks for d in range(2)]
        x = {ch: lvl_ref[0] - a_s[ch] * lvl_ref[1] for ch in chains}
        levels = range(2, n_lvl + 1)
        for i, lvl in enumerate(levels):
            xa = {ch: _dot(x[ch], a_s[ch]).astype(BF16) for ch in chains}
            ys = {ch: _dot(xa[ch], x[ch]) for ch in chains}
            for ch in chains:
                x[ch] = x[ch] - ys[ch].astype(BF16) * lvl_ref[lvl]
            for thunk in fillers[len(fillers) * i // len(levels):
                                 len(fillers) * (i + 1) // len(levels)]:
                thunk()
        for ch in chains:
            sol = _dot(x[ch], rhs_s[ch])
            u0_s[ch] = sol[:, :dk]
            w_s[ch] = sol[:, dk:].astype(BF16)
            nm = _dot(kdt_s[ch], sol.astype(BF16))
            n_s[ch] = nm[:, :dk]
            m_s[ch] = nm[:, dk:].astype(BF16)

    groups = [range(g * unroll, (g + 1) * unroll) for g in range(nc // unroll)]
    pieces = [[thunk for c in grp for thunk in stage_a(c)] for grp in groups]
    for thunk in pieces[0]:
        thunk()
    for g, grp in enumerate(groups):
        stage_b(grp, pieces[g + 1] if g + 1 < len(groups) else [])

    def scan_body(i, carry):
        c_b = jnp.where(i < nl, nl - 1 - i, nc - 1 + nl - i)
        new = []
        for d, c, s in ((0, i, carry[0]), (1, c_b, carry[1])):
            sb = s.astype(BF16)
            st_s[d, c] = sb
            new.append(dec_s[d, c] * s + n_s[d, c] - _dot(m_s[d, c], sb))
        return tuple(new)

    zero = jnp.zeros((dk, dk), F32)
    lax.fori_loop(0, nc, scan_body, (zero, zero))

    def out_body(gi, carry):
        for u in range(unroll):
            c = gi * unroll + u
            rows = pl.ds(pl.multiple_of(c * ck, ck), ck)
            o = jnp.zeros((ck, dk), F32)
            for d in range(2):
                sb = st_s[d, c]
                us = u0_s[d, c] - _dot(w_s[d, c], sb)
                o = o + _dot(qd_s[d, c], sb) + _dot(qk_s[d, c], us.astype(BF16))
            z = z_ref[0, rows, :].astype(F32)
            o_ref[0, rows, :] = (_rms(o, onorm_ref[...]) * _silu(z)).astype(BF16)
        return carry

    lax.fori_loop(0, nc // unroll, out_body, 0)


def gdn_call(qkvz, gates, conv_w, out_norm, *, n_ctx, ck):
    b, t, _ = qkvz.shape
    h, dk = GDN_HEADS, GDN_HEAD_DIM
    nc = t // ck
    unroll = _pick(nc, (3, 2, 1))
    lvl = _level_masks(ck)
    assert 4 * nc <= LANES
    gates = gates.reshape(b, nc, ck, 4, h).transpose(0, 4, 2, 3, 1).reshape(b, h, ck, 4 * nc)
    gates = jnp.pad(gates, ((0, 0), (0, 0), (0, 0), (0, LANES - 4 * nc)))
    col = lambda off: pl.BlockSpec((1, t, dk), lambda b, hd: (b, 0, off + hd))
    cw = lambda off: pl.BlockSpec((CONV_K, dk), lambda b, hd: (0, off + hd))
    both = lambda r, w, dt: pltpu.VMEM((2, nc, r, w), dt)
    kern = functools.partial(_gdn_kernel, n_ctx=n_ctx, ck=ck, unroll=unroll)
    return pl.pallas_call(
        kern,
        grid=(b, h),
        in_specs=[col(0), col(h), col(2 * h), col(3 * h), cw(0), cw(h), cw(2 * h),
                  pl.BlockSpec((1, 1, ck, LANES), lambda b, hd: (b, hd, 0, 0)),
                  _const_spec(lvl.shape),
                  pl.BlockSpec((1, dk), lambda b, hd: (0, 0))],
        out_specs=pl.BlockSpec((1, t, dk), lambda b, hd: (b, 0, hd)),
        out_shape=jax.ShapeDtypeStruct((b, t, h * dk), BF16),
        scratch_shapes=[pltpu.VMEM((ck, LANES), F32), pltpu.VMEM((LANES, ck), F32),
                        both(ck, ck, BF16), both(ck, 2 * dk, BF16), both(dk, ck, BF16),
                        both(ck, dk, F32), both(ck, dk, BF16), both(ck, dk, BF16),
                        both(ck, ck, BF16), both(dk, dk, F32), both(dk, dk, BF16),
                        both(1, dk, F32), both(dk, dk, BF16)],
        compiler_params=_params(("arbitrary", "arbitrary")),
        name="gdn",
    )(qkvz, qkvz, qkvz, qkvz, conv_w, conv_w, conv_w, gates, lvl, out_norm)


def _attn_kernel(q_ref, k_ref, vt_ref, o_ref, *, n_ctx):
    tq = q_ref.shape[1]
    t = k_ref.shape[1]

    def attend(n_keys):
        keys = pl.ds(0, n_keys)
        outs = []
        for g0 in range(0, MLA_HEADS, HEAD_GROUP):
            heads = range(g0, g0 + HEAD_GROUP)
            s_ts = [_dot_nt(k_ref[0, keys, pl.ds(hd * LANES, LANES)],
                            q_ref[0, :, pl.ds(hd * LANES, LANES)]) for hd in heads]
            p_ts = [jnp.exp2(s_t - jnp.max(s_t, axis=0, keepdims=True)) for s_t in s_ts]
            denoms = [jnp.sum(p_t, axis=0, keepdims=True) for p_t in p_ts]
            for hd, p_t, denom in zip(heads, p_ts, denoms):
                o_t = _dot(vt_ref[0, pl.ds(hd * MLA_V, MLA_V), keys], p_t.astype(BF16))
                outs.append(o_t / denom)
        o_ref[0] = jnp.concatenate(outs, axis=0).T.astype(BF16)

    is_ctx = pl.program_id(1) * tq < n_ctx

    @pl.when(is_ctx)
    def _():
        attend(n_ctx)

    @pl.when(jnp.logical_not(is_ctx))
    def _():
        attend(t)


def attn_call(q, k, v_t, *, n_ctx, tq):
    b, t, hw = q.shape
    vw = MLA_HEADS * MLA_V
    kern = functools.partial(_attn_kernel, n_ctx=n_ctx)
    return pl.pallas_call(
        kern,
        grid=(b, t // tq),
        in_specs=[pl.BlockSpec((1, tq, hw), lambda b, i: (b, i, 0)),
                  pl.BlockSpec((1, t, hw), lambda b, i: (b, 0, 0)),
                  pl.BlockSpec((1, v_t.shape[1], t), lambda b, i: (b, 0, 0))],
        out_specs=pl.BlockSpec((1, tq, vw), lambda b, i: (b, i, 0)),
        out_shape=jax.ShapeDtypeStruct((b, t, vw), BF16),
        compiler_params=_params(("arbitrary", "arbitrary")),
        name="attn",
    )(q, k, v_t)


def _head_pad(w, width):
    kdim = w.shape[0]
    w = w.reshape(kdim, MLA_HEADS, width)
    return jnp.pad(w, ((0, 0), (0, 0), (0, LANES - width))).reshape(kdim, MLA_HEADS * LANES)


def _layout_w_in(w):
    d = w.shape[0]
    off_a = 4 * GDN_WIDTH
    off_b = off_a + 2 * GDN_HEADS
    off_cq = off_b + 2 * GDN_HEADS
    off_ckv = off_cq + Q_RANK
    off_kr = off_ckv + KV_RANK
    small = jnp.concatenate(
        [jnp.zeros((d, ROPE_LANE0), w.dtype), w[:, off_kr:off_kr + MLA_ROPE], w[:, off_a:off_cq],
         jnp.zeros((d, LANES - GATE_LANE0 - 4 * GDN_HEADS), w.dtype)], axis=1)
    return jnp.concatenate([w[:, :off_a], w[:, off_cq:off_kr], small], axis=1).astype(BF16)


def _rope_tables(n_ctx, seq):
    rows = seq // GRID_W
    row = jnp.repeat(jnp.arange(rows), GRID_W).astype(F32)
    col = jnp.tile(jnp.arange(GRID_W), rows).astype(F32)
    axis_dim = MLA_ROPE // 2
    inv_freq = jnp.power(ROPE_BASE, -jnp.arange(0, axis_dim, 2, dtype=F32) / axis_dim)
    ang = jnp.concatenate([row[:, None] * inv_freq, col[:, None] * inv_freq], axis=-1)
    cos, sin = jnp.cos(ang), jnp.sin(ang)
    ones = lambda n: jnp.ones((seq, n), F32)
    zeros = lambda n: jnp.zeros((seq, n), F32)
    tail = LANES - GATE_LANE0
    c = jnp.concatenate([ones(ROPE_LANE0), cos, cos, ones(tail)], axis=1)
    s1 = jnp.concatenate([zeros(ROPE_LANE0 + ROPE_HALF), sin, zeros(tail)], axis=1)
    s2 = jnp.concatenate([zeros(ROPE_LANE0), -sin, zeros(ROPE_HALF + tail)], axis=1)
    ctx = lambda fill: jnp.full((n_ctx, LANES), fill, F32)
    return (jnp.concatenate([ctx(1.0), c]), jnp.concatenate([ctx(0.0), s1]),
            jnp.concatenate([ctx(0.0), s2]))


def kernel(x, c, ctx, c_ctx, w_ada, b_ada, norm_pre, norm_post, ffn_w_gate, ffn_w_up, ffn_w_down,
           w_in, gdn_conv, gdn_a_log, gdn_dt_bias, gdn_out_norm, mla_q_norm, mla_kv_norm, mla_w_uq,
           mla_w_ukv, w_out):
    bsz, seq, d = x.shape
    n_ctx = ctx.shape[1]
    t = n_ctx + seq
    depth = w_ada.shape[0]
    tm = _pick(t, (768, 512, 384, 256, 128))
    tq = _pick(n_ctx, (256, 128))
    ck = tq
    assert seq % ck == 0 and seq % GRID_W == 0

    cond_rows = -(-(bsz + 1) // SUBLANES) * SUBLANES
    cond = jnp.concatenate([c, c_ctx[None], jnp.zeros((cond_rows - bsz - 1, d), F32)], axis=0)
    mod = ada_call(cond, w_ada, b_ada).reshape(depth, cond_rows, N_MOD, d)
    rope_c, rope_s1, rope_s2 = _rope_tables(n_ctx, seq)

    ffn_w = (ffn_w_gate.astype(BF16), ffn_w_up.astype(BF16), ffn_w_down.astype(BF16))
    xs = jnp.concatenate([ctx, x], axis=1)
    for l in range(depth):
        row = lambda v: v.reshape(1, -1)
        xs = ffn_call(xs, mod[l], row(norm_pre[l, 0]), row(norm_post[l, 0]), *ffn_w,
                      which=(l, 0), slot=0, n_ctx=n_ctx, tm=tm)

        uq = mla_w_uq[l].reshape(Q_RANK, MLA_HEADS, MLA_NOPE + MLA_ROPE)
        ukv = mla_w_ukv[l].reshape(KV_RANK, MLA_HEADS, MLA_NOPE + MLA_V)
        wuq = _head_pad(uq.reshape(Q_RANK, -1), MLA_NOPE + MLA_ROPE).astype(BF16)
        wuk = _head_pad(ukv[:, :, :MLA_NOPE].reshape(KV_RANK, -1), MLA_NOPE).astype(BF16)
        wuv_t = ukv[:, :, MLA_NOPE:].reshape(KV_RANK, MLA_HEADS * MLA_V).T.astype(BF16)
        gpar = jnp.pad(jnp.stack([gdn_a_log[l].reshape(-1), gdn_dt_bias[l].reshape(-1)]),
                       ((0, 0), (GATE_LANE0, LANES - GATE_LANE0 - 2 * GDN_HEADS)))
        qkvz, small, q, k, v_t = mixin_call(
            xs, mod[l], row(norm_pre[l, 1]), _layout_w_in(w_in[l]), row(mla_q_norm[l]),
            row(mla_kv_norm[l]), wuq, wuk, wuv_t, gpar, rope_c, rope_s1, rope_s2, n_ctx=n_ctx,
            tm=tm)

        gates = small[:, :, GATE_LANE0:GATE_LANE0 + 4 * GDN_HEADS]
        o_gdn = gdn_call(qkvz, gates, gdn_conv[l], row(gdn_out_norm[l]), n_ctx=n_ctx, ck=ck)
        o_mla = attn_call(q, k, v_t, n_ctx=n_ctx, tq=tq)
        xs = mixffn_call(xs, o_gdn, o_mla, mod[l], row(norm_post[l, 1]),
                         w_out[l, :GDN_WIDTH].astype(BF16), w_out[l, GDN_WIDTH:].astype(BF16),
                         row(norm_pre[l, 2]), row(norm_post[l, 2]), *ffn_w, which=(l, 1),
                         n_ctx=n_ctx, tm=tm)
    return xs[:, n_ctx:]
```

```python
import functools
import math

import jax
import jax.numpy as jnp
from jax import lax
from jax.experimental import pallas as pl
from jax.experimental.pallas import tpu as pltpu

F32 = jnp.float32
BF16 = jnp.bfloat16

EPS = 1e-6
N_MOD = 9
GDN_HEADS = 4
GDN_HEAD_DIM = 128
GDN_WIDTH = GDN_HEADS * GDN_HEAD_DIM
CONV_K = 5
CONV_PAD = CONV_K // 2
MLA_HEADS = 8
MLA_NOPE = 64
MLA_ROPE = 32
MLA_V = 64
Q_RANK = 384
KV_RANK = 256
ROPE_BASE = 10000.0
GRID_W = 64
MLA_SCALE = (MLA_NOPE + MLA_ROPE) ** -0.5
LOG2_E = math.log2(math.e)

LANES = 128
SUBLANES = 8
ROPE_HALF = MLA_ROPE // 2
ROPE_LANE0 = MLA_NOPE
GATE_LANE0 = ROPE_LANE0 + MLA_ROPE
HEAD_GROUP = 8
BF16_ROWS = 16
VMEM_LIMIT = 56 * 1024 * 1024


def _pick(n, candidates):
    for cand in candidates:
        if n % cand == 0:
            return cand
    raise ValueError(f"no tile for {n}")


def _sigmoid(x):
    return 1.0 / (1.0 + jnp.exp(-x))


def _silu(x):
    return x * _sigmoid(x)


def _rms(x, gain):
    return x * lax.rsqrt(jnp.mean(x * x, axis=-1, keepdims=True) + EPS) * gain


def _dot(a, b):
    return jnp.dot(a, b, preferred_element_type=F32)


def _dot_nt(a, b):
    return lax.dot_general(a, b, (((1,), (1,)), ((), ())), preferred_element_type=F32)


def _const_spec(shape):
    zeros = (0,) * len(shape)
    return pl.BlockSpec(shape, lambda *_: zeros, pipeline_mode=pl.Buffered(1))


def _params(sem):
    return pltpu.CompilerParams(dimension_semantics=sem, vmem_limit_bytes=VMEM_LIMIT)


def _ada_kernel(s_ref, w_ref, b_ref, o_ref):
    s = _silu(s_ref[...])
    o_ref[0] = jnp.dot(s, w_ref[0], precision=lax.Precision.HIGHEST,
                       preferred_element_type=F32) + b_ref[0]


def ada_call(cond, w_ada, b_ada):
    depth, d, n = w_ada.shape
    rows = cond.shape[0]
    tn = _pick(n, (2304, 1024, 512, 256, 128))
    return pl.pallas_call(
        _ada_kernel,
        grid=(depth, n // tn),
        in_specs=[pl.BlockSpec((rows, d), lambda l, j: (0, 0)),
                  pl.BlockSpec((1, d, tn), lambda l, j: (l, 0, j)),
                  pl.BlockSpec((1, 1, tn), lambda l, j: (l, 0, j))],
        out_specs=pl.BlockSpec((1, rows, tn), lambda l, j: (l, 0, j)),
        out_shape=jax.ShapeDtypeStruct((depth, rows, n), F32),
        compiler_params=_params(("arbitrary", "arbitrary")),
        name="ada",
    )(cond, w_ada, b_ada.reshape(depth, 1, n))


def _mod_vec(ml_ref, mc_ref, slot, j):
    return ml_ref[0, pl.ds(3 * slot + j, 1), :], mc_ref[0, pl.ds(3 * slot + j, 1), :]


def _by_range(ranges, lat_ctx, fn):
    return jnp.concatenate([fn(lo, hi, jnp.where(is_ctx, lat_ctx[1], lat_ctx[0]))
                            for lo, hi, is_ctx in ranges], axis=0)


def _pre_mod(x, ranges, ml_ref, mc_ref, slot, gain):
    xh = x * lax.rsqrt(jnp.mean(x * x, axis=-1, keepdims=True) + EPS)
    mult = tuple(gain * (1.0 + s) for s in _mod_vec(ml_ref, mc_ref, slot, 1))
    shift = _mod_vec(ml_ref, mc_ref, slot, 0)
    scaled = _by_range(ranges, mult, lambda lo, hi, row: xh[lo:hi] * row)
    return _by_range(ranges, shift, lambda lo, hi, row: scaled[lo:hi] + row)


def _post_res(x, y, ranges, ml_ref, mc_ref, slot, gain, weight):
    yh = y * lax.rsqrt(jnp.mean(y * y, axis=-1, keepdims=True) + EPS)
    mult = tuple((weight * gain) * g for g in _mod_vec(ml_ref, mc_ref, slot, 2))
    return _by_range(ranges, mult, lambda lo, hi, row: x[lo:hi] + yh[lo:hi] * row)


def _ctx_ranges(tm, n_ctx):
    first = pl.program_id(1) * tm
    cut = n_ctx % tm
    bounds = [0, tm] if cut == 0 else [0, cut, tm]
    return [(lo, hi, first + lo < n_ctx) for lo, hi in zip(bounds[:-1], bounds[1:])]


def _mod_specs(n_ctx_row, d):
    return [pl.BlockSpec((1, N_MOD, d), lambda b, i: (b, 0, 0)),
            pl.BlockSpec((1, N_MOD, d), lambda b, i: (n_ctx_row, 0, 0))]


def _ffn_tile(x, ranges, ml_ref, mc_ref, gpre_ref, gpost_ref, wg_ref, wu_ref, wd_ref, slot, fc):
    h = _pre_mod(x, ranges, ml_ref, mc_ref, slot, gpre_ref[...]).astype(BF16)
    y = jnp.zeros(x.shape, F32)
    for j in range(wg_ref.shape[1] // fc):
        cols = pl.ds(j * fc, fc)
        act = _silu(_dot(h, wg_ref[:, cols])) * _dot(h, wu_ref[:, cols])
        y = y + _dot(act.astype(BF16), wd_ref[cols, :])
    return _post_res(x, y, ranges, ml_ref, mc_ref, slot, gpost_ref[...], 0.5)


def _ffn_kernel(x_ref, ml_ref, mc_ref, gpre_ref, gpost_ref, wg_ref, wu_ref, wd_ref, o_ref,
                *, slot, n_ctx, fc):
    ranges = _ctx_ranges(x_ref.shape[1], n_ctx)
    o_ref[0] = _ffn_tile(x_ref[0], ranges, ml_ref, mc_ref, gpre_ref, gpost_ref, wg_ref, wu_ref,
                         wd_ref, slot, fc)


def _ffn_weight_specs(d, dff, which):
    pick = lambda r, c: pl.BlockSpec((None, None, r, c), lambda *_: (*which, 0, 0),
                                     pipeline_mode=pl.Buffered(1))
    return [_const_spec((1, d)), _const_spec((1, d)), pick(d, dff), pick(d, dff), pick(dff, d)]


def ffn_call(x, mod, gpre, gpost, wg, wu, wd, *, which, slot, n_ctx, tm):
    b, t, d = x.shape
    dff = wg.shape[-1]
    kern = functools.partial(_ffn_kernel, slot=slot, n_ctx=n_ctx, fc=_pick(dff, (256, 128)))
    tok = pl.BlockSpec((1, tm, d), lambda b, i: (b, i, 0))
    return pl.pallas_call(
        kern,
        grid=(b, t // tm),
        in_specs=[tok] + _mod_specs(b, d) + _ffn_weight_specs(d, dff, which),
        out_specs=tok,
        out_shape=jax.ShapeDtypeStruct(x.shape, F32),
        compiler_params=_params(("arbitrary", "arbitrary")),
        name="ffn",
    )(x, mod, mod, gpre, gpost, wg, wu, wd)


def _mixffn_kernel(x_ref, og_ref, om_ref, ml_ref, mc_ref, gmix_ref, wog_ref, wom_ref,
                   gpre_ref, gpost_ref, wg_ref, wu_ref, wd_ref, o_ref, *, n_ctx, fc):
    ranges = _ctx_ranges(x_ref.shape[1], n_ctx)
    y = _dot(og_ref[0], wog_ref[...]) + _dot(om_ref[0], wom_ref[...])
    x = _post_res(x_ref[0], y, ranges, ml_ref, mc_ref, 1, gmix_ref[...], 1.0)
    o_ref[0] = _ffn_tile(x, ranges, ml_ref, mc_ref, gpre_ref, gpost_ref, wg_ref, wu_ref, wd_ref, 2,
                         fc)


def mixffn_call(x, o_gdn, o_mla, mod, gmix, w_gdn, w_mla, gpre, gpost, wg, wu, wd, *, which, n_ctx,
                tm):
    b, t, d = x.shape
    dff = wg.shape[-1]
    tok = lambda w: pl.BlockSpec((1, tm, w), lambda b, i: (b, i, 0))
    kern = functools.partial(_mixffn_kernel, n_ctx=n_ctx, fc=_pick(dff, (256, 128)))
    return pl.pallas_call(
        kern,
        grid=(b, t // tm),
        in_specs=[tok(d), tok(o_gdn.shape[2]), tok(o_mla.shape[2])] + _mod_specs(b, d)
        + [_const_spec((1, d)), _const_spec(w_gdn.shape), _const_spec(w_mla.shape)]
        + _ffn_weight_specs(d, dff, which),
        out_specs=tok(d),
        out_shape=jax.ShapeDtypeStruct(x.shape, F32),
        compiler_params=_params(("arbitrary", "arbitrary")),
        name="mixffn",
    )(x, o_gdn, o_mla, mod, mod, gmix, w_gdn, w_mla, gpre, gpost, wg, wu, wd)


def _rope(x, c_ref, s1_ref, s2_ref):
    return (x * c_ref[...] + pltpu.roll(x, ROPE_HALF, 1) * s1_ref[...]
            + pltpu.roll(x, LANES - ROPE_HALF, 1) * s2_ref[...])


def _mixin_kernel(x_ref, ml_ref, mc_ref, gpre_ref, win_ref, qn_ref, kvn_ref, wuq_ref, wuk_ref,
                  wuvt_ref, gpar_ref, c_ref, s1_ref, s2_ref,
                  qkvz_ref, small_ref, q_ref, k_ref, vt_ref, *, n_ctx):
    tm = x_ref.shape[1]
    h = _pre_mod(x_ref[0], _ctx_ranges(tm, n_ctx), ml_ref, mc_ref, 1, gpre_ref[...]).astype(BF16)
    n_gdn = qkvz_ref.shape[2]
    qkvz_ref[0] = _dot(h, win_ref[:, pl.ds(0, n_gdn)]).astype(BF16)
    rest = _dot(h, win_ref[:, pl.ds(n_gdn, Q_RANK + KV_RANK + LANES)])
    c_q = rest[:, :Q_RANK]
    c_kv = rest[:, Q_RANK:Q_RANK + KV_RANK]
    small = _rope(rest[:, Q_RANK + KV_RANK:], c_ref, s1_ref, s2_ref)
    lane = lax.broadcasted_iota(jnp.int32, (1, LANES), 1)
    pre = small + gpar_ref[pl.ds(1, 1), :]
    softplus = jnp.maximum(pre, 0.0) + jnp.log(1.0 + jnp.exp(-jnp.abs(pre)))
    log_decay = -jnp.exp(gpar_ref[pl.ds(0, 1), :]) * softplus
    n_dec = 2 * GDN_HEADS
    is_dec = (lane >= GATE_LANE0) & (lane < GATE_LANE0 + n_dec)
    is_beta = (lane >= GATE_LANE0 + n_dec) & (lane < GATE_LANE0 + 2 * n_dec)
    small_ref[0] = jnp.where(is_dec, log_decay, jnp.where(is_beta, _sigmoid(small), small))
    k_rope = jnp.where((lane >= ROPE_LANE0) & (lane < GATE_LANE0), small, 0.0)

    q = _dot(_rms(c_q, qn_ref[...]).astype(BF16), wuq_ref[...]) * (MLA_SCALE * LOG2_E)
    ckv_n = _rms(c_kv, kvn_ref[...]).astype(BF16)
    k = _dot(ckv_n, wuk_ref[...])
    for hd in range(MLA_HEADS):
        cols = pl.ds(hd * LANES, LANES)
        blk = slice(hd * LANES, (hd + 1) * LANES)
        q_ref[0, :, cols] = _rope(q[:, blk], c_ref, s1_ref, s2_ref).astype(BF16)
        k_ref[0, :, cols] = (k[:, blk] + k_rope).astype(BF16)
    vt_ref[0] = _dot_nt(wuvt_ref[...], ckv_n).astype(BF16)


def mixin_call(x, mod, gpre, win, qn, kvn, wuq, wuk, wuv_t, gpar, rope_c, rope_s1, rope_s2, *, n_ctx,
               tm):
    b, t, d = x.shape
    n_gdn = 4 * GDN_WIDTH
    hw = MLA_HEADS * LANES
    vrows = wuv_t.shape[0]
    tok = lambda w: pl.BlockSpec((1, tm, w), lambda b, i: (b, i, 0))
    tab = pl.BlockSpec((tm, LANES), lambda b, i: (i, 0))
    kern = functools.partial(_mixin_kernel, n_ctx=n_ctx)
    return pl.pallas_call(
        kern,
        grid=(b, t // tm),
        in_specs=[tok(d)] + _mod_specs(b, d)
        + [_const_spec((1, d)), _const_spec(win.shape), _const_spec((1, Q_RANK)),
           _const_spec((1, KV_RANK)), _const_spec(wuq.shape), _const_spec(wuk.shape),
           _const_spec(wuv_t.shape), _const_spec(gpar.shape), tab, tab, tab],
        out_specs=[tok(n_gdn), tok(LANES), tok(hw), tok(hw),
                   pl.BlockSpec((1, vrows, tm), lambda b, i: (b, 0, i))],
        out_shape=[jax.ShapeDtypeStruct((b, t, n_gdn), BF16),
                   jax.ShapeDtypeStruct((b, t, LANES), F32),
                   jax.ShapeDtypeStruct((b, t, hw), BF16),
                   jax.ShapeDtypeStruct((b, t, hw), BF16),
                   jax.ShapeDtypeStruct((b, vrows, t), BF16)],
        compiler_params=_params(("arbitrary", "arbitrary")),
        name="mixin",
    )(x, mod, mod, gpre, win, qn, kvn, wuq, wuk, wuv_t, gpar, rope_c, rope_s1, rope_s2)


def _level_masks(ck):
    ii = lax.broadcasted_iota(jnp.int32, (ck, ck), 0)
    jj = lax.broadcasted_iota(jnp.int32, (ck, ck), 1)
    masks = [ii == jj]
    for lvl in range(1, ck.bit_length()):
        masks.append(((ii >> lvl) == (jj >> lvl)) & ((ii >> (lvl - 1)) != (jj >> (lvl - 1))))
    masks.append(ii >= jj)
    return jnp.stack(masks).astype(BF16)


def _gdn_kernel(q_ref, k_ref, v_ref, z_ref, cq_ref, ck_ref, cv_ref, gate_ref, lvl_ref, onorm_ref,
                o_ref, cum_s, cumt_s, a_s, rhs_s, kdt_s, u0_s, w_s, qd_s, qk_s, n_s, m_s, dec_s, st_s,
                *, n_ctx, ck, unroll):
    t = q_ref.shape[1]
    dk = q_ref.shape[2]
    nc = t // ck
    nl = n_ctx // ck
    n_lvl = lvl_ref.shape[0] - 2
    halo = BF16_ROWS

    ii = lax.broadcasted_iota(jnp.int32, (ck, ck), 0)
    jj = lax.broadcasted_iota(jnp.int32, (ck, ck), 1)

    gates = gate_ref[0, 0]
    part, prefix = gates, jnp.zeros(gates.shape, F32)
    for _ in range(3):
        term = part.astype(BF16)
        prefix = prefix + _dot(lvl_ref[n_lvl + 1], term)
        part = part - term.astype(F32)
    totals = prefix[ck - 1:ck, :]
    lane = lax.broadcasted_iota(jnp.int32, (1, gates.shape[1]), 1)
    cum_s[...] = jnp.where(lane < nc, prefix, totals - prefix + gates)
    cumt_s[...] = cum_s[...].T

    def conv(u_ref, w_ref, c):
        r0 = c * ck
        keep_lo = 0.0 if c in (0, nl) else 1.0
        keep_hi = 0.0 if c in (nl - 1, nc - 1) else 1.0
        full = jnp.concatenate(
            [u_ref[0, pl.ds(max(r0 - halo, 0), halo), :].astype(F32) * keep_lo,
             u_ref[0, pl.ds(r0, ck), :].astype(F32),
             u_ref[0, pl.ds(min(r0 + ck, t - halo), halo), :].astype(F32) * keep_hi], axis=0)
        acc = full[halo:halo + ck] * w_ref[pl.ds(CONV_PAD, 1), :]
        for s in range(-CONV_PAD, CONV_PAD + 1):
            if s != 0:
                acc = acc + full[halo + s:halo + s + ck] * w_ref[pl.ds(CONV_PAD + s, 1), :]
        return _silu(acc)

    def l2n(u):
        return u * lax.rsqrt(jnp.sum(u * u, axis=-1, keepdims=True) + EPS)

    def stage_a(c):
        box = {}

        def shared():
            box["q"] = l2n(conv(q_ref, cq_ref, c)) * (dk ** -0.5)
            box["k"] = l2n(conv(k_ref, ck_ref, c))
            box["v"] = conv(v_ref, cv_ref, c)
            box["kb"] = box["k"].astype(BF16)
            box["qk_raw"] = _dot_nt(box["q"].astype(BF16), box["kb"])
            box["k_t"] = box["k"].T

        def direction(d):
            q, k, v, kb = box["q"], box["k"], box["v"], box["kb"]
            incl = (ii >= jj) if d == 0 else (ii <= jj)
            strict = (ii > jj) if d == 0 else (ii < jj)
            col = d * nc + c
            beta = gate_ref[0, 0, :, pl.ds(col + 2 * nc, 1)]
            cum_col = cum_s[:, pl.ds(col, 1)]
            cum_row = cumt_s[pl.ds(col, 1), :]
            total = totals[:, col:col + 1]
            e = jnp.exp(cum_col - cum_row)
            kk = _dot_nt((beta * k).astype(BF16), kb)
            a_s[d, c] = (jnp.where(strict, e, 0.0) * kk).astype(BF16)
            qk_s[d, c] = (jnp.where(incl, e, 0.0) * box["qk_raw"]).astype(BF16)
            e_cum = jnp.exp(cum_col)
            rhs_s[d, c] = jnp.concatenate([beta * v, (beta * e_cum) * k], axis=1).astype(BF16)
            qd_s[d, c] = (q * e_cum).astype(BF16)
            dec_s[d, c] = jnp.broadcast_to(jnp.exp(total), (1, dk))
            kdt_s[d, c] = (box["k_t"] * jnp.exp(total - cum_row)).astype(BF16)

        return [shared, functools.partial(direction, 0), functools.partial(direction, 1)]

    def stage_b(chunks, fillers):
        chains = [(d, c) for c in chunks for d in range(2)]
        x = {ch: lvl_ref[0] - a_s[ch] * lvl_ref[1] for ch in chains}
        levels = range(2, n_lvl + 1)
        for i, lvl in enumerate(levels):
            xa = {ch: _dot(x[ch], a_s[ch]).astype(BF16) for ch in chains}
            ys = {ch: _dot(xa[ch], x[ch]) for ch in chains}
            for ch in chains:
                x[ch] = x[ch] - ys[ch].astype(BF16) * lvl_ref[lvl]
            for thunk in fillers[len(fillers) * i // len(levels):
                                 len(fillers) * (i + 1) // len(levels)]:
                thunk()
        for ch in chains:
            sol = _dot(x[ch], rhs_s[ch])
            u0_s[ch] = sol[:, :dk]
            w_s[ch] = sol[:, dk:].astype(BF16)
            nm = _dot(kdt_s[ch], sol.astype(BF16))
            n_s[ch] = nm[:, :dk]
            m_s[ch] = nm[:, dk:].astype(BF16)

    groups = [range(g * unroll, (g + 1) * unroll) for g in range(nc // unroll)]
    pieces = [[thunk for c in grp for thunk in stage_a(c)] for grp in groups]
    for thunk in pieces[0]:
        thunk()
    for g, grp in enumerate(groups):
        stage_b(grp, pieces[g + 1] if g + 1 < len(groups) else [])

    def scan_body(i, carry):
        c_b = jnp.where(i < nl, nl - 1 - i, nc - 1 + nl - i)
        new = []
        for d, c, s in ((0, i, carry[0]), (1, c_b, carry[1])):
            sb = s.astype(BF16)
            st_s[d, c] = sb
            new.append(dec_s[d, c] * s + n_s[d, c] - _dot(m_s[d, c], sb))
        return tuple(new)

    zero = jnp.zeros((dk, dk), F32)
    lax.fori_loop(0, nc, scan_body, (zero, zero))

    def out_body(gi, carry):
        for u in range(unroll):
            c = gi * unroll + u
            rows = pl.ds(pl.multiple_of(c * ck, ck), ck)
            o = jnp.zeros((ck, dk), F32)
            for d in range(2):
                sb = st_s[d, c]
                us = u0_s[d, c] - _dot(w_s[d, c], sb)
                o = o + _dot(qd_s[d, c], sb) + _dot(qk_s[d, c], us.astype(BF16))
            z = z_ref[0, rows, :].astype(F32)
            o_ref[0, rows, :] = (_rms(o, onorm_ref[...]) * _silu(z)).astype(BF16)
        return carry

    lax.fori_loop(0, nc // unroll, out_body, 0)


def gdn_call(qkvz, gates, conv_w, out_norm, *, n_ctx, ck):
    b, t, _ = qkvz.shape
    h, dk = GDN_HEADS, GDN_HEAD_DIM
    nc = t // ck
    unroll = _pick(nc, (3, 2, 1))
    lvl = _level_masks(ck)
    assert 4 * nc <= LANES
    gates = gates.reshape(b, nc, ck, 4, h).transpose(0, 4, 2, 3, 1).reshape(b, h, ck, 4 * nc)
    gates = jnp.pad(gates, ((0, 0), (0, 0), (0, 0), (0, LANES - 4 * nc)))
    col = lambda off: pl.BlockSpec((1, t, dk), lambda b, hd: (b, 0, off + hd))
    cw = lambda off: pl.BlockSpec((CONV_K, dk), lambda b, hd: (0, off + hd))
    both = lambda r, w, dt: pltpu.VMEM((2, nc, r, w), dt)
    kern = functools.partial(_gdn_kernel, n_ctx=n_ctx, ck=ck, unroll=unroll)
    return pl.pallas_call(
        kern,
        grid=(b, h),
        in_specs=[col(0), col(h), col(2 * h), col(3 * h), cw(0), cw(h), cw(2 * h),
                  pl.BlockSpec((1, 1, ck, LANES), lambda b, hd: (b, hd, 0, 0)),
                  _const_spec(lvl.shape),
                  pl.BlockSpec((1, dk), lambda b, hd: (0, 0))],
        out_specs=pl.BlockSpec((1, t, dk), lambda b, hd: (b, 0, hd)),
        out_shape=jax.ShapeDtypeStruct((b, t, h * dk), BF16),
        scratch_shapes=[pltpu.VMEM((ck, LANES), F32), pltpu.VMEM((LANES, ck), F32),
                        both(ck, ck, BF16), both(ck, 2 * dk, BF16), both(dk, ck, BF16),
                        both(ck, dk, F32), both(ck, dk, BF16), both(ck, dk, BF16),
                        both(ck, ck, BF16), both(dk, dk, F32), both(dk, dk, BF16),
                        both(1, dk, F32), both(dk, dk, BF16)],
        compiler_params=_params(("arbitrary", "arbitrary")),
        name="gdn",
    )(qkvz, qkvz, qkvz, qkvz, conv_w, conv_w, conv_w, gates, lvl, out_norm)


def _attn_kernel(q_ref, k_ref, vt_ref, o_ref, *, n_ctx):
    tq = q_ref.shape[1]
    t = k_ref.shape[1]

    def attend(n_keys):
        keys = pl.ds(0, n_keys)
        outs = []
        for g0 in range(0, MLA_HEADS, HEAD_GROUP):
            heads = range(g0, g0 + HEAD_GROUP)
            s_ts = [_dot_nt(k_ref[0, keys, pl.ds(hd * LANES, LANES)],
                            q_ref[0, :, pl.ds(hd * LANES, LANES)]) for hd in heads]
            p_ts = [jnp.exp2(s_t - jnp.max(s_t, axis=0, keepdims=True)) for s_t in s_ts]
            denoms = [jnp.sum(p_t, axis=0, keepdims=True) for p_t in p_ts]
            for hd, p_t, denom in zip(heads, p_ts, denoms):
                o_t = _dot(vt_ref[0, pl.ds(hd * MLA_V, MLA_V), keys], p_t.astype(BF16))
                outs.append(o_t / denom)
        o_ref[0] = jnp.concatenate(outs, axis=0).T.astype(BF16)

    is_ctx = pl.program_id(1) * tq < n_ctx

    @pl.when(is_ctx)
    def _():
        attend(n_ctx)

    @pl.when(jnp.logical_not(is_ctx))
    def _():
        attend(t)


def attn_call(q, k, v_t, *, n_ctx, tq):
    b, t, hw = q.shape
    vw = MLA_HEADS * MLA_V
    kern = functools.partial(_attn_kernel, n_ctx=n_ctx)
    return pl.pallas_call(
        kern,
        grid=(b, t // tq),
        in_specs=[pl.BlockSpec((1, tq, hw), lambda b, i: (b, i, 0)),
                  pl.BlockSpec((1, t, hw), lambda b, i: (b, 0, 0)),
                  pl.BlockSpec((1, v_t.shape[1], t), lambda b, i: (b, 0, 0))],
        out_specs=pl.BlockSpec((1, tq, vw), lambda b, i: (b, i, 0)),
        out_shape=jax.ShapeDtypeStruct((b, t, vw), BF16),
        compiler_params=_params(("arbitrary", "arbitrary")),
        name="attn",
    )(q, k, v_t)


def _head_pad(w, width):
    kdim = w.shape[0]
    w = w.reshape(kdim, MLA_HEADS, width)
    return jnp.pad(w, ((0, 0), (0, 0), (0, LANES - width))).reshape(kdim, MLA_HEADS * LANES)


def _layout_w_in(w):
    d = w.shape[0]
    off_a = 4 * GDN_WIDTH
    off_b = off_a + 2 * GDN_HEADS
    off_cq = off_b + 2 * GDN_HEADS
    off_ckv = off_cq + Q_RANK
    off_kr = off_ckv + KV_RANK
    small = jnp.concatenate(
        [jnp.zeros((d, ROPE_LANE0), w.dtype), w[:, off_kr:off_kr + MLA_ROPE], w[:, off_a:off_cq],
         jnp.zeros((d, LANES - GATE_LANE0 - 4 * GDN_HEADS), w.dtype)], axis=1)
    return jnp.concatenate([w[:, :off_a], w[:, off_cq:off_kr], small], axis=1).astype(BF16)


def _rope_tables(n_ctx, seq):
    rows = seq // GRID_W
    row = jnp.repeat(jnp.arange(rows), GRID_W).astype(F32)
    col = jnp.tile(jnp.arange(GRID_W), rows).astype(F32)
    axis_dim = MLA_ROPE // 2
    inv_freq = jnp.power(ROPE_BASE, -jnp.arange(0, axis_dim, 2, dtype=F32) / axis_dim)
    ang = jnp.concatenate([row[:, None] * inv_freq, col[:, None] * inv_freq], axis=-1)
    cos, sin = jnp.cos(ang), jnp.sin(ang)
    ones = lambda n: jnp.ones((seq, n), F32)
    zeros = lambda n: jnp.zeros((seq, n), F32)
    tail = LANES - GATE_LANE0
    c = jnp.concatenate([ones(ROPE_LANE0), cos, cos, ones(tail)], axis=1)
    s1 = jnp.concatenate([zeros(ROPE_LANE0 + ROPE_HALF), sin, zeros(tail)], axis=1)
    s2 = jnp.concatenate([zeros(ROPE_LANE0), -sin, zeros(ROPE_HALF + tail)], axis=1)
    ctx = lambda fill: jnp.full((n_ctx, LANES), fill, F32)
    return (jnp.concatenate([ctx(1.0), c]), jnp.concatenate([ctx(0.0), s1]),
            jnp.concatenate([ctx(0.0), s2]))


def kernel(x, c, ctx, c_ctx, w_ada, b_ada, norm_pre, norm_post, ffn_w_gate, ffn_w_up, ffn_w_down,
           w_in, gdn_conv, gdn_a_log, gdn_dt_bias, gdn_out_norm, mla_q_norm, mla_kv_norm, mla_w_uq,
           mla_w_ukv, w_out):
    bsz, seq, d = x.shape
    n_ctx = ctx.shape[1]
    t = n_ctx + seq
    depth = w_ada.shape[0]
    tm = _pick(t, (768, 512, 384, 256, 128))
    tq = _pick(n_ctx, (256, 128))
    ck = tq
    assert seq % ck == 0 and seq % GRID_W == 0

    cond_rows = -(-(bsz + 1) // SUBLANES) * SUBLANES
    cond = jnp.concatenate([c, c_ctx[None], jnp.zeros((cond_rows - bsz - 1, d), F32)], axis=0)
    mod = ada_call(cond, w_ada, b_ada).reshape(depth, cond_rows, N_MOD, d)
    rope_c, rope_s1, rope_s2 = _rope_tables(n_ctx, seq)

    ffn_w = (ffn_w_gate.astype(BF16), ffn_w_up.astype(BF16), ffn_w_down.astype(BF16))
    xs = jnp.concatenate([ctx, x], axis=1)
    for l in range(depth):
        row = lambda v: v.reshape(1, -1)
        xs = ffn_call(xs, mod[l], row(norm_pre[l, 0]), row(norm_post[l, 0]), *ffn_w,
                      which=(l, 0), slot=0, n_ctx=n_ctx, tm=tm)

        uq = mla_w_uq[l].reshape(Q_RANK, MLA_HEADS, MLA_NOPE + MLA_ROPE)
        ukv = mla_w_ukv[l].reshape(KV_RANK, MLA_HEADS, MLA_NOPE + MLA_V)
        wuq = _head_pad(uq.reshape(Q_RANK, -1), MLA_NOPE + MLA_ROPE).astype(BF16)
        wuk = _head_pad(ukv[:, :, :MLA_NOPE].reshape(KV_RANK, -1), MLA_NOPE).astype(BF16)
        wuv_t = ukv[:, :, MLA_NOPE:].reshape(KV_RANK, MLA_HEADS * MLA_V).T.astype(BF16)
        gpar = jnp.pad(jnp.stack([gdn_a_log[l].reshape(-1), gdn_dt_bias[l].reshape(-1)]),
                       ((0, 0), (GATE_LANE0, LANES - GATE_LANE0 - 2 * GDN_HEADS)))
        qkvz, small, q, k, v_t = mixin_call(
            xs, mod[l], row(norm_pre[l, 1]), _layout_w_in(w_in[l]), row(mla_q_norm[l]),
            row(mla_kv_norm[l]), wuq, wuk, wuv_t, gpar, rope_c, rope_s1, rope_s2, n_ctx=n_ctx,
            tm=tm)

        gates = small[:, :, GATE_LANE0:GATE_LANE0 + 4 * GDN_HEADS]
        o_gdn = gdn_call(qkvz, gates, gdn_conv[l], row(gdn_out_norm[l]), n_ctx=n_ctx, ck=ck)
        o_mla = attn_call(q, k, v_t, n_ctx=n_ctx, tq=tq)
        xs = mixffn_call(xs, o_gdn, o_mla, mod[l], row(norm_post[l, 1]),
                         w_out[l, :GDN_WIDTH].astype(BF16), w_out[l, GDN_WIDTH:].astype(BF16),
                         row(norm_pre[l, 2]), row(norm_post[l, 2]), *ffn_w, which=(l, 1),
                         n_ctx=n_ctx, tm=tm)
    return xs[:, n_ctx:]
```

```python
import functools
import math

import jax
import jax.numpy as jnp
from jax import lax
from jax.experimental import pallas as pl
from jax.experimental.pallas import tpu as pltpu

F32 = jnp.float32
BF16 = jnp.bfloat16

EPS = 1e-6
N_MOD = 9
GDN_HEADS = 4
GDN_HEAD_DIM = 128
GDN_WIDTH = GDN_HEADS * GDN_HEAD_DIM
CONV_K = 5
CONV_PAD = CONV_K // 2
MLA_HEADS = 8
MLA_NOPE = 64
MLA_ROPE = 32
MLA_V = 64
Q_RANK = 384
KV_RANK = 256
ROPE_BASE = 10000.0
GRID_W = 64
MLA_SCALE = (MLA_NOPE + MLA_ROPE) ** -0.5
LOG2_E = math.log2(math.e)

LANES = 128
SUBLANES = 8
ROPE_HALF = MLA_ROPE // 2
ROPE_LANE0 = MLA_NOPE
GATE_LANE0 = ROPE_LANE0 + MLA_ROPE
HEAD_GROUP = 8
BF16_ROWS = 16
VMEM_LIMIT = 56 * 1024 * 1024


def _pick(n, candidates):
    for cand in candidates:
        if n % cand == 0:
            return cand
    raise ValueError(f"no tile for {n}")


def _sigmoid(x):
    return 1.0 / (1.0 + jnp.exp(-x))


def _silu(x):
    return x * _sigmoid(x)


def _rms(x, gain):
    return x * lax.rsqrt(jnp.mean(x * x, axis=-1, keepdims=True) + EPS) * gain


def _dot(a, b):
    return jnp.dot(a, b, preferred_element_type=F32)


def _dot_nt(a, b):
    return lax.dot_general(a, b, (((1,), (1,)), ((), ())), preferred_element_type=F32)


def _const_spec(shape):
    zeros = (0,) * len(shape)
    return pl.BlockSpec(shape, lambda *_: zeros, pipeline_mode=pl.Buffered(1))


def _params(sem):
    return pltpu.CompilerParams(dimension_semantics=sem, vmem_limit_bytes=VMEM_LIMIT)


def _ada_kernel(s_ref, w_ref, b_ref, o_ref):
    s = _silu(s_ref[...])
    o_ref[0] = jnp.dot(s, w_ref[0], precision=lax.Precision.HIGHEST,
                       preferred_element_type=F32) + b_ref[0]


def ada_call(cond, w_ada, b_ada):
    depth, d, n = w_ada.shape
    rows = cond.shape[0]
    tn = _pick(n, (2304, 1024, 512, 256, 128))
    return pl.pallas_call(
        _ada_kernel,
        grid=(depth, n // tn),
        in_specs=[pl.BlockSpec((rows, d), lambda l, j: (0, 0)),
                  pl.BlockSpec((1, d, tn), lambda l, j: (l, 0, j)),
                  pl.BlockSpec((1, 1, tn), lambda l, j: (l, 0, j))],
        out_specs=pl.BlockSpec((1, rows, tn), lambda l, j: (l, 0, j)),
        out_shape=jax.ShapeDtypeStruct((depth, rows, n), F32),
        compiler_params=_params(("arbitrary", "arbitrary")),
        name="ada",
    )(cond, w_ada, b_ada.reshape(depth, 1, n))


def _mod_vec(ml_ref, mc_ref, slot, j):
    return ml_ref[0, pl.ds(3 * slot + j, 1), :], mc_ref[0, pl.ds(3 * slot + j, 1), :]


def _by_range(ranges, lat_ctx, fn):
    return jnp.concatenate([fn(lo, hi, jnp.where(is_ctx, lat_ctx[1], lat_ctx[0]))
                            for lo, hi, is_ctx in ranges], axis=0)


def _pre_mod(x, ranges, ml_ref, mc_ref, slot, gain):
    xh = x * lax.rsqrt(jnp.mean(x * x, axis=-1, keepdims=True) + EPS)
    mult = tuple(gain * (1.0 + s) for s in _mod_vec(ml_ref, mc_ref, slot, 1))
    shift = _mod_vec(ml_ref, mc_ref, slot, 0)
    scaled = _by_range(ranges, mult, lambda lo, hi, row: xh[lo:hi] * row)
    return _by_range(ranges, shift, lambda lo, hi, row: scaled[lo:hi] + row)


def _post_res(x, y, ranges, ml_ref, mc_ref, slot, gain, weight):
    yh = y * lax.rsqrt(jnp.mean(y * y, axis=-1, keepdims=True) + EPS)
    mult = tuple((weight * gain) * g for g in _mod_vec(ml_ref, mc_ref, slot, 2))
    return _by_range(ranges, mult, lambda lo, hi, row: x[lo:hi] + yh[lo:hi] * row)


def _ctx_ranges(tm, n_ctx):
    first = pl.program_id(1) * tm
    cut = n_ctx % tm
    bounds = [0, tm] if cut == 0 else [0, cut, tm]
    return [(lo, hi, first + lo < n_ctx) for lo, hi in zip(bounds[:-1], bounds[1:])]


def _mod_specs(n_ctx_row, d):
    return [pl.BlockSpec((1, N_MOD, d), lambda b, i: (b, 0, 0)),
            pl.BlockSpec((1, N_MOD, d), lambda b, i: (n_ctx_row, 0, 0))]


def _ffn_tile(x, ranges, ml_ref, mc_ref, gpre_ref, gpost_ref, wg_ref, wu_ref, wd_ref, slot, fc):
    h = _pre_mod(x, ranges, ml_ref, mc_ref, slot, gpre_ref[...]).astype(BF16)
    y = jnp.zeros(x.shape, F32)
    for j in range(wg_ref.shape[1] // fc):
        cols = pl.ds(j * fc, fc)
        act = _silu(_dot(h, wg_ref[:, cols])) * _dot(h, wu_ref[:, cols])
        y = y + _dot(act.astype(BF16), wd_ref[cols, :])
    return _post_res(x, y, ranges, ml_ref, mc_ref, slot, gpost_ref[...], 0.5)


def _ffn_kernel(x_ref, ml_ref, mc_ref, gpre_ref, gpost_ref, wg_ref, wu_ref, wd_ref, o_ref,
                *, slot, n_ctx, fc):
    ranges = _ctx_ranges(x_ref.shape[1], n_ctx)
    o_ref[0] = _ffn_tile(x_ref[0], ranges, ml_ref, mc_ref, gpre_ref, gpost_ref, wg_ref, wu_ref,
                         wd_ref, slot, fc)


def _ffn_weight_specs(d, dff, which):
    pick = lambda r, c: pl.BlockSpec((None, None, r, c), lambda *_: (*which, 0, 0),
                                     pipeline_mode=pl.Buffered(1))
    return [_const_spec((1, d)), _const_spec((1, d)), pick(d, dff), pick(d, dff), pick(dff, d)]


def ffn_call(x, mod, gpre, gpost, wg, wu, wd, *, which, slot, n_ctx, tm):
    b, t, d = x.shape
    dff = wg.shape[-1]
    kern = functools.partial(_ffn_kernel, slot=slot, n_ctx=n_ctx, fc=_pick(dff, (256, 128)))
    tok = pl.BlockSpec((1, tm, d), lambda b, i: (b, i, 0))
    return pl.pallas_call(
        kern,
        grid=(b, t // tm),
        in_specs=[tok] + _mod_specs(b, d) + _ffn_weight_specs(d, dff, which),
        out_specs=tok,
        out_shape=jax.ShapeDtypeStruct(x.shape, F32),
        compiler_params=_params(("arbitrary", "arbitrary")),
        name="ffn",
    )(x, mod, mod, gpre, gpost, wg, wu, wd)


def _mixffn_kernel(x_ref, og_ref, om_ref, ml_ref, mc_ref, gmix_ref, wog_ref, wom_ref,
                   gpre_ref, gpost_ref, wg_ref, wu_ref, wd_ref, o_ref, *, n_ctx, fc):
    ranges = _ctx_ranges(x_ref.shape[1], n_ctx)
    y = _dot(og_ref[0], wog_ref[...]) + _dot(om_ref[0], wom_ref[...])
    x = _post_res(x_ref[0], y, ranges, ml_ref, mc_ref, 1, gmix_ref[...], 1.0)
    o_ref[0] = _ffn_tile(x, ranges, ml_ref, mc_ref, gpre_ref, gpost_ref, wg_ref, wu_ref, wd_ref, 2,
                         fc)


def mixffn_call(x, o_gdn, o_mla, mod, gmix, w_gdn, w_mla, gpre, gpost, wg, wu, wd, *, which, n_ctx,
                tm):
    b, t, d = x.shape
    dff = wg.shape[-1]
    tok = lambda w: pl.BlockSpec((1, tm, w), lambda b, i: (b, i, 0))
    kern = functools.partial(_mixffn_kernel, n_ctx=n_ctx, fc=_pick(dff, (256, 128)))
    return pl.pallas_call(
        kern,
        grid=(b, t // tm),
        in_specs=[tok(d), tok(o_gdn.shape[2]), tok(o_mla.shape[2])] + _mod_specs(b, d)
        + [_const_spec((1, d)), _const_spec(w_gdn.shape), _const_spec(w_mla.shape)]
        + _ffn_weight_specs(d, dff, which),
        out_specs=tok(d),
        out_shape=jax.ShapeDtypeStruct(x.shape, F32),
        compiler_params=_params(("arbitrary", "arbitrary")),
        name="mixffn",
    )(x, o_gdn, o_mla, mod, mod, gmix, w_gdn, w_mla, gpre, gpost, wg, wu, wd)


def _rope(x, c_ref, s1_ref, s2_ref):
    return (x * c_ref[...] + pltpu.roll(x, ROPE_HALF, 1) * s1_ref[...]
            + pltpu.roll(x, LANES - ROPE_HALF, 1) * s2_ref[...])


def _mixin_kernel(x_ref, ml_ref, mc_ref, gpre_ref, win_ref, qn_ref, kvn_ref, wuq_ref, wuk_ref,
                  wuvt_ref, gpar_ref, c_ref, s1_ref, s2_ref,
                  qkvz_ref, small_ref, q_ref, k_ref, vt_ref, *, n_ctx):
    tm = x_ref.shape[1]
    h = _pre_mod(x_ref[0], _ctx_ranges(tm, n_ctx), ml_ref, mc_ref, 1, gpre_ref[...]).astype(BF16)
    n_gdn = qkvz_ref.shape[2]
    qkvz_ref[0] = _dot(h, win_ref[:, pl.ds(0, n_gdn)]).astype(BF16)
    rest = _dot(h, win_ref[:, pl.ds(n_gdn, Q_RANK + KV_RANK + LANES)])
    c_q = rest[:, :Q_RANK]
    c_kv = rest[:, Q_RANK:Q_RANK + KV_RANK]
    small = _rope(rest[:, Q_RANK + KV_RANK:], c_ref, s1_ref, s2_ref)
    lane = lax.broadcasted_iota(jnp.int32, (1, LANES), 1)
    pre = small + gpar_ref[pl.ds(1, 1), :]
    softplus = jnp.maximum(pre, 0.0) + jnp.log(1.0 + jnp.exp(-jnp.abs(pre)))
    log_decay = -jnp.exp(gpar_ref[pl.ds(0, 1), :]) * softplus
    n_dec = 2 * GDN_HEADS
    is_dec = (lane >= GATE_LANE0) & (lane < GATE_LANE0 + n_dec)
    is_beta = (lane >= GATE_LANE0 + n_dec) & (lane < GATE_LANE0 + 2 * n_dec)
    small_ref[0] = jnp.where(is_dec, log_decay, jnp.where(is_beta, _sigmoid(small), small))
    k_rope = jnp.where((lane >= ROPE_LANE0) & (lane < GATE_LANE0), small, 0.0)

    hw = q_ref.shape[2]
    q = _dot(_rms(c_q, qn_ref[...]).astype(BF16), wuq_ref[...]) * (MLA_SCALE * LOG2_E)
    cos, sin = c_ref[...], s1_ref[...] + s2_ref[...]
    ckv_n = _rms(c_kv, kvn_ref[...]).astype(BF16)
    k = _dot(ckv_n, wuk_ref[...])
    for hd in range(MLA_HEADS):
        cols = pl.ds(hd * LANES, LANES)
        blk = slice(hd * LANES, (hd + 1) * LANES)
        partner = slice(hw + hd * LANES, hw + (hd + 1) * LANES)
        q_ref[0, :, cols] = (q[:, blk] * cos + q[:, partner] * sin).astype(BF16)
        k_ref[0, :, cols] = (k[:, blk] + k_rope).astype(BF16)
    vt_ref[0] = _dot_nt(wuvt_ref[...], ckv_n).astype(BF16)


def mixin_call(x, mod, gpre, win, qn, kvn, wuq, wuk, wuv_t, gpar, rope_c, rope_s1, rope_s2, *, n_ctx,
               tm):
    b, t, d = x.shape
    n_gdn = 4 * GDN_WIDTH
    hw = MLA_HEADS * LANES
    vrows = wuv_t.shape[0]
    tok = lambda w: pl.BlockSpec((1, tm, w), lambda b, i: (b, i, 0))
    tab = pl.BlockSpec((tm, LANES), lambda b, i: (i, 0))
    kern = functools.partial(_mixin_kernel, n_ctx=n_ctx)
    return pl.pallas_call(
        kern,
        grid=(b, t // tm),
        in_specs=[tok(d)] + _mod_specs(b, d)
        + [_const_spec((1, d)), _const_spec(win.shape), _const_spec((1, Q_RANK)),
           _const_spec((1, KV_RANK)), _const_spec(wuq.shape), _const_spec(wuk.shape),
           _const_spec(wuv_t.shape), _const_spec(gpar.shape), tab, tab, tab],
        out_specs=[tok(n_gdn), tok(LANES), tok(hw), tok(hw),
                   pl.BlockSpec((1, vrows, tm), lambda b, i: (b, 0, i))],
        out_shape=[jax.ShapeDtypeStruct((b, t, n_gdn), BF16),
                   jax.ShapeDtypeStruct((b, t, LANES), F32),
                   jax.ShapeDtypeStruct((b, t, hw), BF16),
                   jax.ShapeDtypeStruct((b, t, hw), BF16),
                   jax.ShapeDtypeStruct((b, vrows, t), BF16)],
        compiler_params=_params(("arbitrary", "arbitrary")),
        name="mixin",
    )(x, mod, mod, gpre, win, qn, kvn, wuq, wuk, wuv_t, gpar, rope_c, rope_s1, rope_s2)


def _level_masks(ck):
    ii = lax.broadcasted_iota(jnp.int32, (ck, ck), 0)
    jj = lax.broadcasted_iota(jnp.int32, (ck, ck), 1)
    masks = [ii == jj]
    for lvl in range(1, ck.bit_length()):
        masks.append(((ii >> lvl) == (jj >> lvl)) & ((ii >> (lvl - 1)) != (jj >> (lvl - 1))))
    masks.append(ii >= jj)
    return jnp.stack(masks).astype(BF16)


def _gdn_kernel(q_ref, k_ref, v_ref, z_ref, cq_ref, ck_ref, cv_ref, gate_ref, lvl_ref, onorm_ref,
                o_ref, cum_s, cumt_s, a_s, rhs_s, kdt_s, u0_s, w_s, qd_s, qk_s, n_s, m_s, dec_s, st_s,
                *, n_ctx, ck, unroll):
    t = q_ref.shape[1]
    dk = q_ref.shape[2]
    nc = t // ck
    nl = n_ctx // ck
    n_lvl = lvl_ref.shape[0] - 2
    halo = BF16_ROWS

    ii = lax.broadcasted_iota(jnp.int32, (ck, ck), 0)
    jj = lax.broadcasted_iota(jnp.int32, (ck, ck), 1)

    gates = gate_ref[0, 0]
    part, prefix = gates, jnp.zeros(gates.shape, F32)
    for _ in range(3):
        term = part.astype(BF16)
        prefix = prefix + _dot(lvl_ref[n_lvl + 1], term)
        part = part - term.astype(F32)
    totals = prefix[ck - 1:ck, :]
    lane = lax.broadcasted_iota(jnp.int32, (1, gates.shape[1]), 1)
    cum_s[...] = jnp.where(lane < nc, prefix, totals - prefix + gates)
    cumt_s[...] = cum_s[...].T

    def conv(u_ref, w_ref, c):
        r0 = c * ck
        keep_lo = 0.0 if c in (0, nl) else 1.0
        keep_hi = 0.0 if c in (nl - 1, nc - 1) else 1.0
        full = jnp.concatenate(
            [u_ref[0, pl.ds(max(r0 - halo, 0), halo), :].astype(F32) * keep_lo,
             u_ref[0, pl.ds(r0, ck), :].astype(F32),
             u_ref[0, pl.ds(min(r0 + ck, t - halo), halo), :].astype(F32) * keep_hi], axis=0)
        acc = full[halo:halo + ck] * w_ref[pl.ds(CONV_PAD, 1), :]
        for s in range(-CONV_PAD, CONV_PAD + 1):
            if s != 0:
                acc = acc + full[halo + s:halo + s + ck] * w_ref[pl.ds(CONV_PAD + s, 1), :]
        return _silu(acc)

    def l2n(u):
        return u * lax.rsqrt(jnp.sum(u * u, axis=-1, keepdims=True) + EPS)

    def stage_a(c):
        box = {}

        def shared():
            box["q"] = l2n(conv(q_ref, cq_ref, c)) * (dk ** -0.5)
            box["k"] = l2n(conv(k_ref, ck_ref, c))
            box["v"] = conv(v_ref, cv_ref, c)
            box["kb"] = box["k"].astype(BF16)
            box["qk_raw"] = _dot_nt(box["q"].astype(BF16), box["kb"])
            box["k_t"] = box["k"].T

        def direction(d):
            q, k, v, kb = box["q"], box["k"], box["v"], box["kb"]
            incl = (ii >= jj) if d == 0 else (ii <= jj)
            strict = (ii > jj) if d == 0 else (ii < jj)
            col = d * nc + c
            beta = gate_ref[0, 0, :, pl.ds(col + 2 * nc, 1)]
            cum_col = cum_s[:, pl.ds(col, 1)]
            cum_row = cumt_s[pl.ds(col, 1), :]
            total = totals[:, col:col + 1]
            e = jnp.exp(cum_col - cum_row)
            kk = _dot_nt((beta * k).astype(BF16), kb)
            a_s[d, c] = (jnp.where(strict, e, 0.0) * kk).astype(BF16)
            qk_s[d, c] = (jnp.where(incl, e, 0.0) * box["qk_raw"]).astype(BF16)
            e_cum = jnp.exp(cum_col)
            rhs_s[d, c] = jnp.concatenate([beta * v, (beta * e_cum) * k], axis=1).astype(BF16)
            qd_s[d, c] = (q * e_cum).astype(BF16)
            dec_s[d, c] = jnp.broadcast_to(jnp.exp(total), (1, dk))
            kdt_s[d, c] = (box["k_t"] * jnp.exp(total - cum_row)).astype(BF16)

        return [shared, functools.partial(direction, 0), functools.partial(direction, 1)]

    def stage_b(chunks, fillers):
        chains = [(d, c) for c in chunks for d in range(2)]
        x = {ch: lvl_ref[0] - a_s[ch] * lvl_ref[1] for ch in chains}
        levels = range(2, n_lvl + 1)
        for i, lvl in enumerate(levels):
            xa = {ch: _dot(x[ch], a_s[ch]).astype(BF16) for ch in chains}
            ys = {ch: _dot(xa[ch], x[ch]) for ch in chains}
            for ch in chains:
                x[ch] = x[ch] - ys[ch].astype(BF16) * lvl_ref[lvl]
            for thunk in fillers[len(fillers) * i // len(levels):
                                 len(fillers) * (i + 1) // len(levels)]:
                thunk()
        for ch in chains:
            sol = _dot(x[ch], rhs_s[ch])
            u0_s[ch] = sol[:, :dk]
            w_s[ch] = sol[:, dk:].astype(BF16)
            nm = _dot(kdt_s[ch], sol.astype(BF16))
            n_s[ch] = nm[:, :dk]
            m_s[ch] = nm[:, dk:].astype(BF16)

    groups = [range(g * unroll, (g + 1) * unroll) for g in range(nc // unroll)]
    pieces = [[thunk for c in grp for thunk in stage_a(c)] for grp in groups]
    for thunk in pieces[0]:
        thunk()
    for g, grp in enumerate(groups):
        stage_b(grp, pieces[g + 1] if g + 1 < len(groups) else [])

    def scan_body(i, carry):
        c_b = jnp.where(i < nl, nl - 1 - i, nc - 1 + nl - i)
        new = []
        for d, c, s in ((0, i, carry[0]), (1, c_b, carry[1])):
            sb = s.astype(BF16)
            st_s[d, c] = sb
            new.append(dec_s[d, c] * s + n_s[d, c] - _dot(m_s[d, c], sb))
        return tuple(new)

    zero = jnp.zeros((dk, dk), F32)
    lax.fori_loop(0, nc, scan_body, (zero, zero))

    def out_body(gi, carry):
        for u in range(unroll):
            c = gi * unroll + u
            rows = pl.ds(pl.multiple_of(c * ck, ck), ck)
            o = jnp.zeros((ck, dk), F32)
            for d in range(2):
                sb = st_s[d, c]
                us = u0_s[d, c] - _dot(w_s[d, c], sb)
                o = o + _dot(qd_s[d, c], sb) + _dot(qk_s[d, c], us.astype(BF16))
            z = z_ref[0, rows, :].astype(F32)
            o_ref[0, rows, :] = (_rms(o, onorm_ref[...]) * _silu(z)).astype(BF16)
        return carry

    lax.fori_loop(0, nc // unroll, out_body, 0)


def gdn_call(qkvz, gates, conv_w, out_norm, *, n_ctx, ck):
    b, t, _ = qkvz.shape
    h, dk = GDN_HEADS, GDN_HEAD_DIM
    nc = t // ck
    unroll = _pick(nc, (3, 2, 1))
    lvl = _level_masks(ck)
    assert 4 * nc <= LANES
    gates = gates.reshape(b, nc, ck, 4, h).transpose(0, 4, 2, 3, 1).reshape(b, h, ck, 4 * nc)
    gates = jnp.pad(gates, ((0, 0), (0, 0), (0, 0), (0, LANES - 4 * nc)))
    col = lambda off: pl.BlockSpec((1, t, dk), lambda b, hd: (b, 0, off + hd))
    cw = lambda off: pl.BlockSpec((CONV_K, dk), lambda b, hd: (0, off + hd))
    both = lambda r, w, dt: pltpu.VMEM((2, nc, r, w), dt)
    kern = functools.partial(_gdn_kernel, n_ctx=n_ctx, ck=ck, unroll=unroll)
    return pl.pallas_call(
        kern,
        grid=(b, h),
        in_specs=[col(0), col(h), col(2 * h), col(3 * h), cw(0), cw(h), cw(2 * h),
                  pl.BlockSpec((1, 1, ck, LANES), lambda b, hd: (b, hd, 0, 0)),
                  _const_spec(lvl.shape),
                  pl.BlockSpec((1, dk), lambda b, hd: (0, 0))],
        out_specs=pl.BlockSpec((1, t, dk), lambda b, hd: (b, 0, hd)),
        out_shape=jax.ShapeDtypeStruct((b, t, h * dk), BF16),
        scratch_shapes=[pltpu.VMEM((ck, LANES), F32), pltpu.VMEM((LANES, ck), F32),
                        both(ck, ck, BF16), both(ck, 2 * dk, BF16), both(dk, ck, BF16),
                        both(ck, dk, F32), both(ck, dk, BF16), both(ck, dk, BF16),
                        both(ck, ck, BF16), both(dk, dk, F32), both(dk, dk, BF16),
                        both(1, dk, F32), both(dk, dk, BF16)],
        compiler_params=_params(("arbitrary", "arbitrary")),
        name="gdn",
    )(qkvz, qkvz, qkvz, qkvz, conv_w, conv_w, conv_w, gates, lvl, out_norm)


def _attn_kernel(q_ref, k_ref, vt_ref, o_ref, *, n_ctx):
    tq = q_ref.shape[1]
    t = k_ref.shape[1]

    def attend(n_keys):
        keys = pl.ds(0, n_keys)
        outs = []
        for g0 in range(0, MLA_HEADS, HEAD_GROUP):
            heads = range(g0, g0 + HEAD_GROUP)
            s_ts = [_dot_nt(k_ref[0, keys, pl.ds(hd * LANES, LANES)],
                            q_ref[0, :, pl.ds(hd * LANES, LANES)]) for hd in heads]
            p_ts = [jnp.exp2(s_t - jnp.max(s_t, axis=0, keepdims=True)) for s_t in s_ts]
            denoms = [jnp.sum(p_t, axis=0, keepdims=True) for p_t in p_ts]
            for hd, p_t, denom in zip(heads, p_ts, denoms):
                o_t = _dot(vt_ref[0, pl.ds(hd * MLA_V, MLA_V), keys], p_t.astype(BF16))
                outs.append(o_t / denom)
        o_ref[0] = jnp.concatenate(outs, axis=0).T.astype(BF16)

    is_ctx = pl.program_id(1) * tq < n_ctx

    @pl.when(is_ctx)
    def _():
        attend(n_ctx)

    @pl.when(jnp.logical_not(is_ctx))
    def _():
        attend(t)


def attn_call(q, k, v_t, *, n_ctx, tq):
    b, t, hw = q.shape
    vw = MLA_HEADS * MLA_V
    kern = functools.partial(_attn_kernel, n_ctx=n_ctx)
    return pl.pallas_call(
        kern,
        grid=(b, t // tq),
        in_specs=[pl.BlockSpec((1, tq, hw), lambda b, i: (b, i, 0)),
                  pl.BlockSpec((1, t, hw), lambda b, i: (b, 0, 0)),
                  pl.BlockSpec((1, v_t.shape[1], t), lambda b, i: (b, 0, 0))],
        out_specs=pl.BlockSpec((1, tq, vw), lambda b, i: (b, i, 0)),
        out_shape=jax.ShapeDtypeStruct((b, t, vw), BF16),
        compiler_params=_params(("arbitrary", "arbitrary")),
        name="attn",
    )(q, k, v_t)


def _head_pad(w, width):
    kdim = w.shape[0]
    w = w.reshape(kdim, MLA_HEADS, width)
    return jnp.pad(w, ((0, 0), (0, 0), (0, LANES - width))).reshape(kdim, MLA_HEADS * LANES)


def _layout_w_in(w):
    d = w.shape[0]
    off_a = 4 * GDN_WIDTH
    off_b = off_a + 2 * GDN_HEADS
    off_cq = off_b + 2 * GDN_HEADS
    off_ckv = off_cq + Q_RANK
    off_kr = off_ckv + KV_RANK
    small = jnp.concatenate(
        [jnp.zeros((d, ROPE_LANE0), w.dtype), w[:, off_kr:off_kr + MLA_ROPE], w[:, off_a:off_cq],
         jnp.zeros((d, LANES - GATE_LANE0 - 4 * GDN_HEADS), w.dtype)], axis=1)
    return jnp.concatenate([w[:, :off_a], w[:, off_cq:off_kr], small], axis=1).astype(BF16)


def _rope_tables(n_ctx, seq):
    rows = seq // GRID_W
    row = jnp.repeat(jnp.arange(rows), GRID_W).astype(F32)
    col = jnp.tile(jnp.arange(GRID_W), rows).astype(F32)
    axis_dim = MLA_ROPE // 2
    inv_freq = jnp.power(ROPE_BASE, -jnp.arange(0, axis_dim, 2, dtype=F32) / axis_dim)
    ang = jnp.concatenate([row[:, None] * inv_freq, col[:, None] * inv_freq], axis=-1)
    cos, sin = jnp.cos(ang), jnp.sin(ang)
    ones = lambda n: jnp.ones((seq, n), F32)
    zeros = lambda n: jnp.zeros((seq, n), F32)
    tail = LANES - GATE_LANE0
    c = jnp.concatenate([ones(ROPE_LANE0), cos, cos, ones(tail)], axis=1)
    s1 = jnp.concatenate([zeros(ROPE_LANE0 + ROPE_HALF), sin, zeros(tail)], axis=1)
    s2 = jnp.concatenate([zeros(ROPE_LANE0), -sin, zeros(ROPE_HALF + tail)], axis=1)
    ctx = lambda fill: jnp.full((n_ctx, LANES), fill, F32)
    return (jnp.concatenate([ctx(1.0), c]), jnp.concatenate([ctx(0.0), s1]),
            jnp.concatenate([ctx(0.0), s2]))


def kernel(x, c, ctx, c_ctx, w_ada, b_ada, norm_pre, norm_post, ffn_w_gate, ffn_w_up, ffn_w_down,
           w_in, gdn_conv, gdn_a_log, gdn_dt_bias, gdn_out_norm, mla_q_norm, mla_kv_norm, mla_w_uq,
           mla_w_ukv, w_out):
    bsz, seq, d = x.shape
    n_ctx = ctx.shape[1]
    t = n_ctx + seq
    depth = w_ada.shape[0]
    tm = _pick(t, (768, 512, 384, 256, 128))
    tq = _pick(n_ctx, (256, 128))
    ck = tq
    assert seq % ck == 0 and seq % GRID_W == 0

    cond_rows = -(-(bsz + 1) // SUBLANES) * SUBLANES
    cond = jnp.concatenate([c, c_ctx[None], jnp.zeros((cond_rows - bsz - 1, d), F32)], axis=0)
    mod = ada_call(cond, w_ada, b_ada).reshape(depth, cond_rows, N_MOD, d)
    rope_c, rope_s1, rope_s2 = _rope_tables(n_ctx, seq)

    ffn_w = (ffn_w_gate.astype(BF16), ffn_w_up.astype(BF16), ffn_w_down.astype(BF16))
    xs = jnp.concatenate([ctx, x], axis=1)
    for l in range(depth):
        row = lambda v: v.reshape(1, -1)
        xs = ffn_call(xs, mod[l], row(norm_pre[l, 0]), row(norm_post[l, 0]), *ffn_w,
                      which=(l, 0), slot=0, n_ctx=n_ctx, tm=tm)

        uq = mla_w_uq[l].reshape(Q_RANK, MLA_HEADS, MLA_NOPE + MLA_ROPE)
        ukv = mla_w_ukv[l].reshape(KV_RANK, MLA_HEADS, MLA_NOPE + MLA_V)
        x1, x2 = uq[:, :, MLA_NOPE:MLA_NOPE + ROPE_HALF], uq[:, :, MLA_NOPE + ROPE_HALF:]
        partner = jnp.pad(jnp.concatenate([x2, x1], axis=-1),
                          ((0, 0), (0, 0), (ROPE_LANE0, LANES - GATE_LANE0)))
        wuq = jnp.concatenate([_head_pad(uq.reshape(Q_RANK, -1), MLA_NOPE + MLA_ROPE),
                               partner.reshape(Q_RANK, -1)], axis=1).astype(BF16)
        wuk = _head_pad(ukv[:, :, :MLA_NOPE].reshape(KV_RANK, -1), MLA_NOPE).astype(BF16)
        wuv_t = ukv[:, :, MLA_NOPE:].reshape(KV_RANK, MLA_HEADS * MLA_V).T.astype(BF16)
        gpar = jnp.pad(jnp.stack([gdn_a_log[l].reshape(-1), gdn_dt_bias[l].reshape(-1)]),
                       ((0, 0), (GATE_LANE0, LANES - GATE_LANE0 - 2 * GDN_HEADS)))
        qkvz, small, q, k, v_t = mixin_call(
            xs, mod[l], row(norm_pre[l, 1]), _layout_w_in(w_in[l]), row(mla_q_norm[l]),
            row(mla_kv_norm[l]), wuq, wuk, wuv_t, gpar, rope_c, rope_s1, rope_s2, n_ctx=n_ctx,
            tm=tm)

        gates = small[:, :, GATE_LANE0:GATE_LANE0 + 4 * GDN_HEADS]
        o_gdn = gdn_call(qkvz, gates, gdn_conv[l], row(gdn_out_norm[l]), n_ctx=n_ctx, ck=ck)
        o_mla = attn_call(q, k, v_t, n_ctx=n_ctx, tq=tq)
        xs = mixffn_call(xs, o_gdn, o_mla, mod[l], row(norm_post[l, 1]),
                         w_out[l, :GDN_WIDTH].astype(BF16), w_out[l, GDN_WIDTH:].astype(BF16),
                         row(norm_pre[l, 2]), row(norm_post[l, 2]), *ffn_w, which=(l, 1),
                         n_ctx=n_ctx, tm=tm)
    return xs[:, n_ctx:]
```

```python
import functools
import math

import jax
import jax.numpy as jnp
from jax import lax
from jax.experimental import pallas as pl
from jax.experimental.pallas import tpu as pltpu

F32 = jnp.float32
BF16 = jnp.bfloat16

EPS = 1e-6
N_MOD = 9
GDN_HEADS = 4
GDN_HEAD_DIM = 128
GDN_WIDTH = GDN_HEADS * GDN_HEAD_DIM
CONV_K = 5
CONV_PAD = CONV_K // 2
MLA_HEADS = 8
MLA_NOPE = 64
MLA_ROPE = 32
MLA_V = 64
Q_RANK = 384
KV_RANK = 256
ROPE_BASE = 10000.0
GRID_W = 64
MLA_SCALE = (MLA_NOPE + MLA_ROPE) ** -0.5
LOG2_E = math.log2(math.e)

LANES = 128
SUBLANES = 8
ROPE_HALF = MLA_ROPE // 2
ROPE_LANE0 = MLA_NOPE
GATE_LANE0 = ROPE_LANE0 + MLA_ROPE
HEAD_GROUP = 8
BF16_ROWS = 16
VMEM_LIMIT = 56 * 1024 * 1024


def _pick(n, candidates):
    for cand in candidates:
        if n % cand == 0:
            return cand
    raise ValueError(f"no tile for {n}")


def _sigmoid(x):
    return 1.0 / (1.0 + jnp.exp(-x))


def _silu(x):
    return x * _sigmoid(x)


def _rms(x, gain):
    return x * lax.rsqrt(jnp.mean(x * x, axis=-1, keepdims=True) + EPS) * gain


def _dot(a, b):
    return jnp.dot(a, b, preferred_element_type=F32)


def _dot_nt(a, b):
    return lax.dot_general(a, b, (((1,), (1,)), ((), ())), preferred_element_type=F32)


def _const_spec(shape):
    zeros = (0,) * len(shape)
    return pl.BlockSpec(shape, lambda *_: zeros, pipeline_mode=pl.Buffered(1))


def _params(sem):
    return pltpu.CompilerParams(dimension_semantics=sem, vmem_limit_bytes=VMEM_LIMIT)


def _ada_kernel(s_ref, w_ref, b_ref, o_ref):
    s = _silu(s_ref[...])
    o_ref[0] = jnp.dot(s, w_ref[0], precision=lax.Precision.HIGHEST,
                       preferred_element_type=F32) + b_ref[0]


def ada_call(cond, w_ada, b_ada):
    depth, d, n = w_ada.shape
    rows = cond.shape[0]
    tn = _pick(n, (2304, 1024, 512, 256, 128))
    return pl.pallas_call(
        _ada_kernel,
        grid=(depth, n // tn),
        in_specs=[pl.BlockSpec((rows, d), lambda l, j: (0, 0)),
                  pl.BlockSpec((1, d, tn), lambda l, j: (l, 0, j)),
                  pl.BlockSpec((1, 1, tn), lambda l, j: (l, 0, j))],
        out_specs=pl.BlockSpec((1, rows, tn), lambda l, j: (l, 0, j)),
        out_shape=jax.ShapeDtypeStruct((depth, rows, n), F32),
        compiler_params=_params(("arbitrary", "arbitrary")),
        name="ada",
    )(cond, w_ada, b_ada.reshape(depth, 1, n))


def _mod_vec(ml_ref, mc_ref, slot, j):
    return ml_ref[0, pl.ds(3 * slot + j, 1), :], mc_ref[0, pl.ds(3 * slot + j, 1), :]


def _by_range(ranges, lat_ctx, fn):
    return jnp.concatenate([fn(lo, hi, jnp.where(is_ctx, lat_ctx[1], lat_ctx[0]))
                            for lo, hi, is_ctx in ranges], axis=0)


def _pre_mod(x, ranges, ml_ref, mc_ref, slot, gain):
    xh = x * lax.rsqrt(jnp.mean(x * x, axis=-1, keepdims=True) + EPS)
    mult = tuple(gain * (1.0 + s) for s in _mod_vec(ml_ref, mc_ref, slot, 1))
    shift = _mod_vec(ml_ref, mc_ref, slot, 0)
    scaled = _by_range(ranges, mult, lambda lo, hi, row: xh[lo:hi] * row)
    return _by_range(ranges, shift, lambda lo, hi, row: scaled[lo:hi] + row)


def _post_res(x, y, ranges, ml_ref, mc_ref, slot, gain, weight):
    yh = y * lax.rsqrt(jnp.mean(y * y, axis=-1, keepdims=True) + EPS)
    mult = tuple((weight * gain) * g for g in _mod_vec(ml_ref, mc_ref, slot, 2))
    return _by_range(ranges, mult, lambda lo, hi, row: x[lo:hi] + yh[lo:hi] * row)


def _ctx_ranges(tm, n_ctx):
    first = pl.program_id(1) * tm
    cut = n_ctx % tm
    bounds = [0, tm] if cut == 0 else [0, cut, tm]
    return [(lo, hi, first + lo < n_ctx) for lo, hi in zip(bounds[:-1], bounds[1:])]


def _mod_specs(n_ctx_row, d):
    return [pl.BlockSpec((1, N_MOD, d), lambda b, i: (b, 0, 0)),
            pl.BlockSpec((1, N_MOD, d), lambda b, i: (n_ctx_row, 0, 0))]


def _ffn_tile(x, ranges, ml_ref, mc_ref, gpre_ref, gpost_ref, wg_ref, wu_ref, wd_ref, slot, fc):
    h = _pre_mod(x, ranges, ml_ref, mc_ref, slot, gpre_ref[...]).astype(BF16)
    y = jnp.zeros(x.shape, F32)
    for j in range(wg_ref.shape[1] // fc):
        cols = pl.ds(j * fc, fc)
        act = _silu(_dot(h, wg_ref[:, cols])) * _dot(h, wu_ref[:, cols])
        y = y + _dot(act.astype(BF16), wd_ref[cols, :])
    return _post_res(x, y, ranges, ml_ref, mc_ref, slot, gpost_ref[...], 0.5)


def _ffn_kernel(x_ref, ml_ref, mc_ref, gpre_ref, gpost_ref, wg_ref, wu_ref, wd_ref, o_ref,
                *, slot, n_ctx, fc):
    ranges = _ctx_ranges(x_ref.shape[1], n_ctx)
    o_ref[0] = _ffn_tile(x_ref[0], ranges, ml_ref, mc_ref, gpre_ref, gpost_ref, wg_ref, wu_ref,
                         wd_ref, slot, fc)


def _ffn_weight_specs(d, dff, which):
    pick = lambda r, c: pl.BlockSpec((None, None, r, c), lambda *_: (*which, 0, 0),
                                     pipeline_mode=pl.Buffered(1))
    return [_const_spec((1, d)), _const_spec((1, d)), pick(d, dff), pick(d, dff), pick(dff, d)]


def ffn_call(x, mod, gpre, gpost, wg, wu, wd, *, which, slot, n_ctx, tm):
    b, t, d = x.shape
    dff = wg.shape[-1]
    kern = functools.partial(_ffn_kernel, slot=slot, n_ctx=n_ctx, fc=_pick(dff, (256, 128)))
    tok = pl.BlockSpec((1, tm, d), lambda b, i: (b, i, 0))
    return pl.pallas_call(
        kern,
        grid=(b, t // tm),
        in_specs=[tok] + _mod_specs(b, d) + _ffn_weight_specs(d, dff, which),
        out_specs=tok,
        out_shape=jax.ShapeDtypeStruct(x.shape, F32),
        compiler_params=_params(("arbitrary", "arbitrary")),
        name="ffn",
    )(x, mod, mod, gpre, gpost, wg, wu, wd)


def _mixffn_kernel(x_ref, og_ref, om_ref, ml_ref, mc_ref, gmix_ref, wog_ref, wom_ref,
                   gpre_ref, gpost_ref, wg_ref, wu_ref, wd_ref, o_ref, *, n_ctx, fc):
    ranges = _ctx_ranges(x_ref.shape[1], n_ctx)
    y = _dot(og_ref[0], wog_ref[...]) + _dot(om_ref[0], wom_ref[...])
    x = _post_res(x_ref[0], y, ranges, ml_ref, mc_ref, 1, gmix_ref[...], 1.0)
    o_ref[0] = _ffn_tile(x, ranges, ml_ref, mc_ref, gpre_ref, gpost_ref, wg_ref, wu_ref, wd_ref, 2,
                         fc)


def mixffn_call(x, o_gdn, o_mla, mod, gmix, w_gdn, w_mla, gpre, gpost, wg, wu, wd, *, which, n_ctx,
                tm):
    b, t, d = x.shape
    dff = wg.shape[-1]
    tok = lambda w: pl.BlockSpec((1, tm, w), lambda b, i: (b, i, 0))
    kern = functools.partial(_mixffn_kernel, n_ctx=n_ctx, fc=_pick(dff, (256, 128)))
    return pl.pallas_call(
        kern,
        grid=(b, t // tm),
        in_specs=[tok(d), tok(o_gdn.shape[2]), tok(o_mla.shape[2])] + _mod_specs(b, d)
        + [_const_spec((1, d)), _const_spec(w_gdn.shape), _const_spec(w_mla.shape)]
        + _ffn_weight_specs(d, dff, which),
        out_specs=tok(d),
        out_shape=jax.ShapeDtypeStruct(x.shape, F32),
        compiler_params=_params(("arbitrary", "arbitrary")),
        name="mixffn",
    )(x, o_gdn, o_mla, mod, mod, gmix, w_gdn, w_mla, gpre, gpost, wg, wu, wd)


def _rope(x, c_ref, s1_ref, s2_ref):
    return (x * c_ref[...] + pltpu.roll(x, ROPE_HALF, 1) * s1_ref[...]
            + pltpu.roll(x, LANES - ROPE_HALF, 1) * s2_ref[...])


def _mixin_kernel(x_ref, ml_ref, mc_ref, gpre_ref, win_ref, qn_ref, kvn_ref, wuq_ref, wuk_ref,
                  wuvt_ref, gpar_ref, c_ref, s1_ref, s2_ref,
                  qkvz_ref, small_ref, q_ref, k_ref, vt_ref, *, n_ctx):
    tm = x_ref.shape[1]
    h = _pre_mod(x_ref[0], _ctx_ranges(tm, n_ctx), ml_ref, mc_ref, 1, gpre_ref[...]).astype(BF16)
    n_gdn = qkvz_ref.shape[2]
    rest = _dot(h, win_ref[:, pl.ds(n_gdn, Q_RANK + KV_RANK + LANES)])
    c_q = rest[:, :Q_RANK]
    c_kv = rest[:, Q_RANK:Q_RANK + KV_RANK]
    small = _rope(rest[:, Q_RANK + KV_RANK:], c_ref, s1_ref, s2_ref)
    lane = lax.broadcasted_iota(jnp.int32, (1, LANES), 1)
    pre = small + gpar_ref[pl.ds(1, 1), :]
    softplus = jnp.maximum(pre, 0.0) + jnp.log(1.0 + jnp.exp(-jnp.abs(pre)))
    log_decay = -jnp.exp(gpar_ref[pl.ds(0, 1), :]) * softplus
    n_dec = 2 * GDN_HEADS
    is_dec = (lane >= GATE_LANE0) & (lane < GATE_LANE0 + n_dec)
    is_beta = (lane >= GATE_LANE0 + n_dec) & (lane < GATE_LANE0 + 2 * n_dec)
    small_ref[0] = jnp.where(is_dec, log_decay, jnp.where(is_beta, _sigmoid(small), small))
    k_rope = jnp.where((lane >= ROPE_LANE0) & (lane < GATE_LANE0), small, 0.0)

    q = _dot(_rms(c_q, qn_ref[...]).astype(BF16), wuq_ref[...]) * (MLA_SCALE * LOG2_E)
    ckv_n = _rms(c_kv, kvn_ref[...]).astype(BF16)
    k = _dot(ckv_n, wuk_ref[...])
    for hd in range(MLA_HEADS):
        cols = pl.ds(hd * LANES, LANES)
        blk = slice(hd * LANES, (hd + 1) * LANES)
        q_ref[0, :, cols] = _rope(q[:, blk], c_ref, s1_ref, s2_ref).astype(BF16)
        k_ref[0, :, cols] = (k[:, blk] + k_rope).astype(BF16)
    vt_ref[0] = _dot_nt(wuvt_ref[...], ckv_n).astype(BF16)
    qkvz_ref[0] = _dot(h, win_ref[:, pl.ds(0, n_gdn)]).astype(BF16)


def mixin_call(x, mod, gpre, win, qn, kvn, wuq, wuk, wuv_t, gpar, rope_c, rope_s1, rope_s2, *, n_ctx,
               tm):
    b, t, d = x.shape
    n_gdn = 4 * GDN_WIDTH
    hw = MLA_HEADS * LANES
    vrows = wuv_t.shape[0]
    tok = lambda w: pl.BlockSpec((1, tm, w), lambda b, i: (b, i, 0))
    tab = pl.BlockSpec((tm, LANES), lambda b, i: (i, 0))
    kern = functools.partial(_mixin_kernel, n_ctx=n_ctx)
    return pl.pallas_call(
        kern,
        grid=(b, t // tm),
        in_specs=[tok(d)] + _mod_specs(b, d)
        + [_const_spec((1, d)), _const_spec(win.shape), _const_spec((1, Q_RANK)),
           _const_spec((1, KV_RANK)), _const_spec(wuq.shape), _const_spec(wuk.shape),
           _const_spec(wuv_t.shape), _const_spec(gpar.shape), tab, tab, tab],
        out_specs=[tok(n_gdn), tok(LANES), tok(hw), tok(hw),
                   pl.BlockSpec((1, vrows, tm), lambda b, i: (b, 0, i))],
        out_shape=[jax.ShapeDtypeStruct((b, t, n_gdn), BF16),
                   jax.ShapeDtypeStruct((b, t, LANES), F32),
                   jax.ShapeDtypeStruct((b, t, hw), BF16),
                   jax.ShapeDtypeStruct((b, t, hw), BF16),
                   jax.ShapeDtypeStruct((b, vrows, t), BF16)],
        compiler_params=_params(("arbitrary", "arbitrary")),
        name="mixin",
    )(x, mod, mod, gpre, win, qn, kvn, wuq, wuk, wuv_t, gpar, rope_c, rope_s1, rope_s2)


def _level_masks(ck):
    ii = lax.broadcasted_iota(jnp.int32, (ck, ck), 0)
    jj = lax.broadcasted_iota(jnp.int32, (ck, ck), 1)
    masks = [ii == jj]
    for lvl in range(1, ck.bit_length()):
        masks.append(((ii >> lvl) == (jj >> lvl)) & ((ii >> (lvl - 1)) != (jj >> (lvl - 1))))
    masks.append(ii >= jj)
    return jnp.stack(masks).astype(BF16)


def _gdn_kernel(q_ref, k_ref, v_ref, z_ref, cq_ref, ck_ref, cv_ref, gate_ref, lvl_ref, onorm_ref,
                o_ref, cum_s, cumt_s, a_s, rhs_s, kdt_s, u0_s, w_s, qd_s, qk_s, n_s, m_s, dec_s, st_s,
                *, n_ctx, ck, unroll):
    t = q_ref.shape[1]
    dk = q_ref.shape[2]
    nc = t // ck
    nl = n_ctx // ck
    n_lvl = lvl_ref.shape[0] - 2
    halo = BF16_ROWS

    ii = lax.broadcasted_iota(jnp.int32, (ck, ck), 0)
    jj = lax.broadcasted_iota(jnp.int32, (ck, ck), 1)

    gates = gate_ref[0, 0]
    part, prefix = gates, jnp.zeros(gates.shape, F32)
    for _ in range(3):
        term = part.astype(BF16)
        prefix = prefix + _dot(lvl_ref[n_lvl + 1], term)
        part = part - term.astype(F32)
    totals = prefix[ck - 1:ck, :]
    lane = lax.broadcasted_iota(jnp.int32, (1, gates.shape[1]), 1)
    cum_s[...] = jnp.where(lane < nc, prefix, totals - prefix + gates)
    cumt_s[...] = cum_s[...].T

    def conv(u_ref, w_ref, c):
        r0 = c * ck
        keep_lo = 0.0 if c in (0, nl) else 1.0
        keep_hi = 0.0 if c in (nl - 1, nc - 1) else 1.0
        full = jnp.concatenate(
            [u_ref[0, pl.ds(max(r0 - halo, 0), halo), :].astype(F32) * keep_lo,
             u_ref[0, pl.ds(r0, ck), :].astype(F32),
             u_ref[0, pl.ds(min(r0 + ck, t - halo), halo), :].astype(F32) * keep_hi], axis=0)
        acc = full[halo:halo + ck] * w_ref[pl.ds(CONV_PAD, 1), :]
        for s in range(-CONV_PAD, CONV_PAD + 1):
            if s != 0:
                acc = acc + full[halo + s:halo + s + ck] * w_ref[pl.ds(CONV_PAD + s, 1), :]
        return _silu(acc)

    def l2n(u):
        return u * lax.rsqrt(jnp.sum(u * u, axis=-1, keepdims=True) + EPS)

    def stage_a(c):
        box = {}

        def shared():
            box["q"] = l2n(conv(q_ref, cq_ref, c)) * (dk ** -0.5)
            box["k"] = l2n(conv(k_ref, ck_ref, c))
            box["v"] = conv(v_ref, cv_ref, c)
            box["kb"] = box["k"].astype(BF16)
            box["qk_raw"] = _dot_nt(box["q"].astype(BF16), box["kb"])
            box["k_t"] = box["k"].T

        def direction(d):
            q, k, v, kb = box["q"], box["k"], box["v"], box["kb"]
            incl = (ii >= jj) if d == 0 else (ii <= jj)
            strict = (ii > jj) if d == 0 else (ii < jj)
            col = d * nc + c
            beta = gate_ref[0, 0, :, pl.ds(col + 2 * nc, 1)]
            cum_col = cum_s[:, pl.ds(col, 1)]
            cum_row = cumt_s[pl.ds(col, 1), :]
            total = totals[:, col:col + 1]
            e = jnp.exp(cum_col - cum_row)
            kk = _dot_nt((beta * k).astype(BF16), kb)
            a_s[d, c] = (jnp.where(strict, e, 0.0) * kk).astype(BF16)
            qk_s[d, c] = (jnp.where(incl, e, 0.0) * box["qk_raw"]).astype(BF16)
            e_cum = jnp.exp(cum_col)
            rhs_s[d, c] = jnp.concatenate([beta * v, (beta * e_cum) * k], axis=1).astype(BF16)
            qd_s[d, c] = (q * e_cum).astype(BF16)
            dec_s[d, c] = jnp.broadcast_to(jnp.exp(total), (1, dk))
            kdt_s[d, c] = (box["k_t"] * jnp.exp(total - cum_row)).astype(BF16)

        return [shared, functools.partial(direction, 0), functools.partial(direction, 1)]

    def stage_b(chunks, fillers):
        chains = [(d, c) for c in chunks for d in range(2)]
        x = {ch: lvl_ref[0] - a_s[ch] * lvl_ref[1] for ch in chains}
        levels = range(2, n_lvl + 1)
        for i, lvl in enumerate(levels):
            xa = {ch: _dot(x[ch], a_s[ch]).astype(BF16) for ch in chains}
            ys = {ch: _dot(xa[ch], x[ch]) for ch in chains}
            for ch in chains:
                x[ch] = x[ch] - ys[ch].astype(BF16) * lvl_ref[lvl]
            for thunk in fillers[len(fillers) * i // len(levels):
                                 len(fillers) * (i + 1) // len(levels)]:
                thunk()
        for ch in chains:
            sol = _dot(x[ch], rhs_s[ch])
            u0_s[ch] = sol[:, :dk]
            w_s[ch] = sol[:, dk:].astype(BF16)
            nm = _dot(kdt_s[ch], sol.astype(BF16))
            n_s[ch] = nm[:, :dk]
            m_s[ch] = nm[:, dk:].astype(BF16)

    groups = [range(g * unroll, (g + 1) * unroll) for g in range(nc // unroll)]
    pieces = [[thunk for c in grp for thunk in stage_a(c)] for grp in groups]
    for thunk in pieces[0]:
        thunk()
    for g, grp in enumerate(groups):
        stage_b(grp, pieces[g + 1] if g + 1 < len(groups) else [])

    def scan_body(i, carry):
        c_b = jnp.where(i < nl, nl - 1 - i, nc - 1 + nl - i)
        new = []
        for d, c, s in ((0, i, carry[0]), (1, c_b, carry[1])):
            sb = s.astype(BF16)
            st_s[d, c] = sb
            new.append(dec_s[d, c] * s + n_s[d, c] - _dot(m_s[d, c], sb))
        return tuple(new)

    zero = jnp.zeros((dk, dk), F32)
    lax.fori_loop(0, nc, scan_body, (zero, zero))

    def out_body(gi, carry):
        for u in range(unroll):
            c = gi * unroll + u
            rows = pl.ds(pl.multiple_of(c * ck, ck), ck)
            o = jnp.zeros((ck, dk), F32)
            for d in range(2):
                sb = st_s[d, c]
                us = u0_s[d, c] - _dot(w_s[d, c], sb)
                o = o + _dot(qd_s[d, c], sb) + _dot(qk_s[d, c], us.astype(BF16))
            z = z_ref[0, rows, :].astype(F32)
            o_ref[0, rows, :] = (_rms(o, onorm_ref[...]) * _silu(z)).astype(BF16)
        return carry

    lax.fori_loop(0, nc // unroll, out_body, 0)


def gdn_call(qkvz, gates, conv_w, out_norm, *, n_ctx, ck):
    b, t, _ = qkvz.shape
    h, dk = GDN_HEADS, GDN_HEAD_DIM
    nc = t // ck
    unroll = _pick(nc, (3, 2, 1))
    lvl = _level_masks(ck)
    assert 4 * nc <= LANES
    gates = gates.reshape(b, nc, ck, 4, h).transpose(0, 4, 2, 3, 1).reshape(b, h, ck, 4 * nc)
    gates = jnp.pad(gates, ((0, 0), (0, 0), (0, 0), (0, LANES - 4 * nc)))
    col = lambda off: pl.BlockSpec((1, t, dk), lambda b, hd: (b, 0, off + hd))
    cw = lambda off: pl.BlockSpec((CONV_K, dk), lambda b, hd: (0, off + hd))
    both = lambda r, w, dt: pltpu.VMEM((2, nc, r, w), dt)
    kern = functools.partial(_gdn_kernel, n_ctx=n_ctx, ck=ck, unroll=unroll)
    return pl.pallas_call(
        kern,
        grid=(b, h),
        in_specs=[col(0), col(h), col(2 * h), col(3 * h), cw(0), cw(h), cw(2 * h),
                  pl.BlockSpec((1, 1, ck, LANES), lambda b, hd: (b, hd, 0, 0)),
                  _const_spec(lvl.shape),
                  pl.BlockSpec((1, dk), lambda b, hd: (0, 0))],
        out_specs=pl.BlockSpec((1, t, dk), lambda b, hd: (b, 0, hd)),
        out_shape=jax.ShapeDtypeStruct((b, t, h * dk), BF16),
        scratch_shapes=[pltpu.VMEM((ck, LANES), F32), pltpu.VMEM((LANES, ck), F32),
                        both(ck, ck, BF16), both(ck, 2 * dk, BF16), both(dk, ck, BF16),
                        both(ck, dk, F32), both(ck, dk, BF16), both(ck, dk, BF16),
                        both(ck, ck, BF16), both(dk, dk, F32), both(dk, dk, BF16),
                        both(1, dk, F32), both(dk, dk, BF16)],
        compiler_params=_params(("arbitrary", "arbitrary")),
        name="gdn",
    )(qkvz, qkvz, qkvz, qkvz, conv_w, conv_w, conv_w, gates, lvl, out_norm)


def _attn_kernel(q_ref, k_ref, vt_ref, o_ref, *, n_ctx):
    tq = q_ref.shape[1]
    t = k_ref.shape[1]

    def attend(n_keys):
        keys = pl.ds(0, n_keys)
        outs = []
        for g0 in range(0, MLA_HEADS, HEAD_GROUP):
            heads = range(g0, g0 + HEAD_GROUP)
            s_ts = [_dot_nt(k_ref[0, keys, pl.ds(hd * LANES, LANES)],
                            q_ref[0, :, pl.ds(hd * LANES, LANES)]) for hd in heads]
            p_ts = [jnp.exp2(s_t - jnp.max(s_t, axis=0, keepdims=True)) for s_t in s_ts]
            denoms = [jnp.sum(p_t, axis=0, keepdims=True) for p_t in p_ts]
            for hd, p_t, denom in zip(heads, p_ts, denoms):
                o_t = _dot(vt_ref[0, pl.ds(hd * MLA_V, MLA_V), keys], p_t.astype(BF16))
                outs.append(o_t / denom)
        o_ref[0] = jnp.concatenate(outs, axis=0).T.astype(BF16)

    is_ctx = pl.program_id(1) * tq < n_ctx

    @pl.when(is_ctx)
    def _():
        attend(n_ctx)

    @pl.when(jnp.logical_not(is_ctx))
    def _():
        attend(t)


def attn_call(q, k, v_t, *, n_ctx, tq):
    b, t, hw = q.shape
    vw = MLA_HEADS * MLA_V
    kern = functools.partial(_attn_kernel, n_ctx=n_ctx)
    return pl.pallas_call(
        kern,
        grid=(b, t // tq),
        in_specs=[pl.BlockSpec((1, tq, hw), lambda b, i: (b, i, 0)),
                  pl.BlockSpec((1, t, hw), lambda b, i: (b, 0, 0)),
                  pl.BlockSpec((1, v_t.shape[1], t), lambda b, i: (b, 0, 0))],
        out_specs=pl.BlockSpec((1, tq, vw), lambda b, i: (b, i, 0)),
        out_shape=jax.ShapeDtypeStruct((b, t, vw), BF16),
        compiler_params=_params(("arbitrary", "arbitrary")),
        name="attn",
    )(q, k, v_t)


def _head_pad(w, width):
    kdim = w.shape[0]
    w = w.reshape(kdim, MLA_HEADS, width)
    return jnp.pad(w, ((0, 0), (0, 0), (0, LANES - width))).reshape(kdim, MLA_HEADS * LANES)


def _layout_w_in(w):
    d = w.shape[0]
    off_a = 4 * GDN_WIDTH
    off_b = off_a + 2 * GDN_HEADS
    off_cq = off_b + 2 * GDN_HEADS
    off_ckv = off_cq + Q_RANK
    off_kr = off_ckv + KV_RANK
    small = jnp.concatenate(
        [jnp.zeros((d, ROPE_LANE0), w.dtype), w[:, off_kr:off_kr + MLA_ROPE], w[:, off_a:off_cq],
         jnp.zeros((d, LANES - GATE_LANE0 - 4 * GDN_HEADS), w.dtype)], axis=1)
    return jnp.concatenate([w[:, :off_a], w[:, off_cq:off_kr], small], axis=1).astype(BF16)


def _rope_tables(n_ctx, seq):
    rows = seq // GRID_W
    row = jnp.repeat(jnp.arange(rows), GRID_W).astype(F32)
    col = jnp.tile(jnp.arange(GRID_W), rows).astype(F32)
    axis_dim = MLA_ROPE // 2
    inv_freq = jnp.power(ROPE_BASE, -jnp.arange(0, axis_dim, 2, dtype=F32) / axis_dim)
    ang = jnp.concatenate([row[:, None] * inv_freq, col[:, None] * inv_freq], axis=-1)
    cos, sin = jnp.cos(ang), jnp.sin(ang)
    ones = lambda n: jnp.ones((seq, n), F32)
    zeros = lambda n: jnp.zeros((seq, n), F32)
    tail = LANES - GATE_LANE0
    c = jnp.concatenate([ones(ROPE_LANE0), cos, cos, ones(tail)], axis=1)
    s1 = jnp.concatenate([zeros(ROPE_LANE0 + ROPE_HALF), sin, zeros(tail)], axis=1)
    s2 = jnp.concatenate([zeros(ROPE_LANE0), -sin, zeros(ROPE_HALF + tail)], axis=1)
    ctx = lambda fill: jnp.full((n_ctx, LANES), fill, F32)
    return (jnp.concatenate([ctx(1.0), c]), jnp.concatenate([ctx(0.0), s1]),
            jnp.concatenate([ctx(0.0), s2]))


def kernel(x, c, ctx, c_ctx, w_ada, b_ada, norm_pre, norm_post, ffn_w_gate, ffn_w_up, ffn_w_down,
           w_in, gdn_conv, gdn_a_log, gdn_dt_bias, gdn_out_norm, mla_q_norm, mla_kv_norm, mla_w_uq,
           mla_w_ukv, w_out):
    bsz, seq, d = x.shape
    n_ctx = ctx.shape[1]
    t = n_ctx + seq
    depth = w_ada.shape[0]
    tm = _pick(t, (768, 512, 384, 256, 128))
    tq = _pick(n_ctx, (256, 128))
    ck = tq
    assert seq % ck == 0 and seq % GRID_W == 0

    cond_rows = -(-(bsz + 1) // SUBLANES) * SUBLANES
    cond = jnp.concatenate([c, c_ctx[None], jnp.zeros((cond_rows - bsz - 1, d), F32)], axis=0)
    mod = ada_call(cond, w_ada, b_ada).reshape(depth, cond_rows, N_MOD, d)
    rope_c, rope_s1, rope_s2 = _rope_tables(n_ctx, seq)

    ffn_w = (ffn_w_gate.astype(BF16), ffn_w_up.astype(BF16), ffn_w_down.astype(BF16))
    xs = jnp.concatenate([ctx, x], axis=1)
    for l in range(depth):
        row = lambda v: v.reshape(1, -1)
        xs = ffn_call(xs, mod[l], row(norm_pre[l, 0]), row(norm_post[l, 0]), *ffn_w,
                      which=(l, 0), slot=0, n_ctx=n_ctx, tm=tm)

        uq = mla_w_uq[l].reshape(Q_RANK, MLA_HEADS, MLA_NOPE + MLA_ROPE)
        ukv = mla_w_ukv[l].reshape(KV_RANK, MLA_HEADS, MLA_NOPE + MLA_V)
        wuq = _head_pad(uq.reshape(Q_RANK, -1), MLA_NOPE + MLA_ROPE).astype(BF16)
        wuk = _head_pad(ukv[:, :, :MLA_NOPE].reshape(KV_RANK, -1), MLA_NOPE).astype(BF16)
        wuv_t = ukv[:, :, MLA_NOPE:].reshape(KV_RANK, MLA_HEADS * MLA_V).T.astype(BF16)
        gpar = jnp.pad(jnp.stack([gdn_a_log[l].reshape(-1), gdn_dt_bias[l].reshape(-1)]),
                       ((0, 0), (GATE_LANE0, LANES - GATE_LANE0 - 2 * GDN_HEADS)))
        qkvz, small, q, k, v_t = mixin_call(
            xs, mod[l], row(norm_pre[l, 1]), _layout_w_in(w_in[l]), row(mla_q_norm[l]),
            row(mla_kv_norm[l]), wuq, wuk, wuv_t, gpar, rope_c, rope_s1, rope_s2, n_ctx=n_ctx,
            tm=tm)

        gates = small[:, :, GATE_LANE0:GATE_LANE0 + 4 * GDN_HEADS]
        o_gdn = gdn_call(qkvz, gates, gdn_conv[l], row(gdn_out_norm[l]), n_ctx=n_ctx, ck=ck)
        o_mla = attn_call(q, k, v_t, n_ctx=n_ctx, tq=tq)
        xs = mixffn_call(xs, o_gdn, o_mla, mod[l], row(norm_post[l, 1]),
                         w_out[l, :GDN_WIDTH].astype(BF16), w_out[l, GDN_WIDTH:].astype(BF16),
                         row(norm_pre[l, 2]), row(norm_post[l, 2]), *ffn_w, which=(l, 1),
                         n_ctx=n_ctx, tm=tm)
    return xs[:, n_ctx:]
```
